```python
import jax, jax.numpy as jnp
from jax import lax
import numpy as np

D_MODEL = 1024
BATCH = 16
SEQ = 2048
DEPTH = 4

CTX_LEN = 256
GRID_W = 64
HEAD_DIM = 64
N_Q_HEADS = 12
N_KV_HEADS = 4
Q_BLOCK = 128
ROPE_THETA = 10000.0
AXIS_DIM = HEAD_DIM // 2
N_FOURIER_GROUPS = 4
FOURIER_GROUP_DIM = 64
CHUNK = 128
N_SGU_GROUPS = 4
SGU_GROUP_DIM = 128
CONV_WIDTH = 31
N_CONV_GROUPS = 4
CONV_GROUP_DIM = 128
N_EXPERT_GROUPS = 4
EXPERTS_PER_GROUP = 8
N_EXPERTS = N_EXPERT_GROUPS * EXPERTS_PER_GROUP
TOP_K = 2
D_EXPERT = 256

Q_W = N_Q_HEADS * HEAD_DIM
KV_W = N_KV_HEADS * HEAD_DIM
F_W = N_FOURIER_GROUPS * FOURIER_GROUP_DIM
AB_IN = Q_W + 2 * KV_W + F_W
AB_OUT = Q_W + F_W
C_W = N_SGU_GROUPS * SGU_GROUP_DIM
D_W = N_CONV_GROUPS * CONV_GROUP_DIM
CD_IN = 2 * C_W + 2 * D_W
CD_OUT = C_W + D_W
N_EVEN = (DEPTH + 1) // 2
N_ODD = DEPTH // 2
ALPHA = (2 * DEPTH) ** 0.25
BETA = (8 * DEPTH) ** -0.25
EPS = 1e-6

kernel_name = "hybrid_dit_attn_fourier_sgu_conv_hmoe"


def layer_norm(x, g, b):
    xf = x.astype(jnp.float32)
    mu = jnp.mean(xf, axis=-1, keepdims=True)
    var = jnp.mean(jnp.square(xf - mu), axis=-1, keepdims=True)
    return ((xf - mu) * lax.rsqrt(var + EPS) * g + b).astype(x.dtype)


def rms_norm(x, g):
    xf = x.astype(jnp.float32)
    return (xf * lax.rsqrt(jnp.mean(jnp.square(xf), axis=-1, keepdims=True) + EPS) * g).astype(x.dtype)


def rope_tables(n):
    rows = n // GRID_W
    row = jnp.repeat(jnp.arange(rows, dtype=jnp.float32), GRID_W)
    col = jnp.tile(jnp.arange(GRID_W, dtype=jnp.float32), rows)
    inv_freq = 1.0 / (ROPE_THETA ** (jnp.arange(0, AXIS_DIM, 2, dtype=jnp.float32) / AXIS_DIM))
    ang = jnp.concatenate([row[:, None] * inv_freq, col[:, None] * inv_freq], axis=-1)
    return jnp.cos(ang), jnp.sin(ang)


def apply_rope(x, cos, sin):
    xr = x.astype(jnp.float32).reshape(x.shape[:-1] + (HEAD_DIM // 2, 2))
    x0, x1 = xr[..., 0], xr[..., 1]
    cs, sn = cos[None, :, None, :], sin[None, :, None, :]
    out = jnp.stack([x0 * cs - x1 * sn, x0 * sn + x1 * cs], axis=-1)
    return out.reshape(x.shape).astype(x.dtype)


def gqa_core(q, k, v):
    s = jnp.einsum('bqgrd,bkgd->bgrqk', q, k, preferred_element_type=jnp.float32) * (HEAD_DIM ** -0.5)
    p = jax.nn.softmax(s, axis=-1).astype(v.dtype)
    return jnp.einsum('bgrqk,bkgd->bqgrd', p, v)


def block_attention(q, k, v):
    bsz, n = q.shape[:2]
    rep = N_Q_HEADS // N_KV_HEADS
    qb = q.reshape(bsz, n // Q_BLOCK, Q_BLOCK, N_KV_HEADS, rep, HEAD_DIM).transpose(1, 0, 2, 3, 4, 5)
    out = lax.map(lambda blk: gqa_core(blk, k, v), qb)
    return out.transpose(1, 0, 2, 3, 4, 5).reshape(bsz, n, Q_W)


def fourier_mix(f, w, b):
    bsz, n, _ = f.shape
    fg = f.reshape(bsz, n, N_FOURIER_GROUPS, FOURIER_GROUP_DIM).astype(jnp.float32)
    mixed = jnp.fft.fft2(fg, axes=(1, 3), norm='ortho').real.astype(f.dtype)
    return (jnp.einsum('bngc,gcd->bngd', mixed, w) + b).reshape(bsz, n, F_W)


def mixer_ab(h_lat, h_ctx, ctx_out, w_in, w_out, q_gain, k_gain, w_f, b_f):
    bsz, n, _ = h_lat.shape
    lc = h_ctx.shape[1]
    rep = N_Q_HEADS // N_KV_HEADS
    q, k, v, f = jnp.split(h_lat @ w_in, [Q_W, Q_W + KV_W, Q_W + 2 * KV_W], axis=-1)
    cos, sin = rope_tables(n)
    q = apply_rope(rms_norm(q.reshape(bsz, n, N_Q_HEADS, HEAD_DIM), q_gain), cos, sin)
    k = apply_rope(rms_norm(k.reshape(bsz, n, N_KV_HEADS, HEAD_DIM), k_gain), cos, sin)
    v = v.reshape(bsz, n, N_KV_HEADS, HEAD_DIM)
    if ctx_out:
        qc, kc, vc, fc = jnp.split(h_ctx @ w_in, [Q_W, Q_W + KV_W, Q_W + 2 * KV_W], axis=-1)
    else:
        kc, vc = jnp.split(h_ctx @ w_in[:, Q_W:Q_W + 2 * KV_W], [KV_W], axis=-1)
    kc = rms_norm(kc.reshape(bsz, lc, N_KV_HEADS, HEAD_DIM), k_gain)
    vc = vc.reshape(bsz, lc, N_KV_HEADS, HEAD_DIM)
    k_all = jnp.concatenate([kc, k], axis=1)
    v_all = jnp.concatenate([vc, v], axis=1)
    a_lat = block_attention(q, k_all, v_all)
    y_lat = jnp.concatenate([a_lat, fourier_mix(f, w_f, b_f)], axis=-1) @ w_out
    if not ctx_out:
        return y_lat, None
    qc = rms_norm(qc.reshape(bsz, lc, N_Q_HEADS, HEAD_DIM), q_gain).reshape(bsz, lc, N_KV_HEADS, rep, HEAD_DIM)
    a_ctx = gqa_core(qc, kc, vc).reshape(bsz, lc, Q_W)
    y_ctx = jnp.concatenate([a_ctx, fourier_mix(fc, w_f, b_f)], axis=-1) @ w_out
    return y_lat, y_ctx


def mixer_cd(h, w_in, w_out, sgu_g, sgu_b, w_sp, b_sp, conv_w, conv_b, cn_g, cn_b):
    bsz, n, _ = h.shape
    p = h @ w_in
    u, v = jnp.split(jax.nn.gelu(p[..., :2 * C_W]), 2, axis=-1)
    a, gate = jnp.split(p[..., 2 * C_W:], 2, axis=-1)
    vg = layer_norm(v.reshape(bsz, n // CHUNK, CHUNK, N_SGU_GROUPS, SGU_GROUP_DIM), sgu_g, sgu_b)
    sv = jnp.einsum('gpq,bnqgc->bnpgc', w_sp, vg) + b_sp.T[:, :, None]
    y_c = u * sv.reshape(bsz, n, C_W)
    glu = a * jax.nn.sigmoid(gate)
    dw = lax.conv_general_dilated(glu, conv_w[:, None, :], window_strides=(1,),
                                  padding=[(CONV_WIDTH // 2, CONV_WIDTH // 2)],
                                  dimension_numbers=('NWC', 'WIO', 'NWC'),
                                  feature_group_count=D_W) + conv_b
    dn = layer_norm(dw.reshape(bsz, n, N_CONV_GROUPS, CONV_GROUP_DIM), cn_g, cn_b).reshape(bsz, n, D_W)
    y_d = jax.nn.silu(dn)
    return jnp.concatenate([y_c, y_d], axis=-1) @ w_out


def hier_moe(t, w_group, b_group, w_router, b_router, w_gate, w_up, w_down):
    g_logits = (t @ w_group + b_group).astype(jnp.float32)
    g_prob = jax.nn.softmax(g_logits, axis=-1)
    g_idx = jnp.argmax(g_logits, axis=-1)
    g_w = jnp.take_along_axis(g_prob, g_idx[:, None], axis=-1)
    e_logits = (jnp.einsum('td,gde->tge', t, w_router) + b_router).astype(jnp.float32)
    e_sel = jnp.take_along_axis(e_logits, g_idx[:, None, None], axis=1)[:, 0]
    top_v, top_i = lax.top_k(e_sel, TOP_K)
    top_w = jax.nn.softmax(top_v, axis=-1) * g_w
    expert_id = g_idx[:, None] * EXPERTS_PER_GROUP + top_i
    combine = jnp.einsum('tk,tke->te', top_w,
                         jax.nn.one_hot(expert_id, N_EXPERTS, dtype=jnp.float32)).astype(t.dtype)
    out = jnp.zeros_like(t)
    for e in range(N_EXPERTS):
        hid = jax.nn.silu(t @ w_gate[e]) * (t @ w_up[e])
        out = out + combine[:, e:e + 1] * (hid @ w_down[e])
    return out


def setup_inputs(seed: int = 0) -> dict:
    key = jax.random.key(seed)
    ks = iter(jax.random.split(key, 40))
    f32 = jnp.float32

    def nrm(shape, scale):
        return jax.random.normal(next(ks), shape, f32) * scale

    def gain(shape):
        return 1.0 + nrm(shape, 0.05)

    return {
        "x": nrm((BATCH, SEQ, D_MODEL), 1.0),
        "c": nrm((BATCH, D_MODEL), 1.0),
        "ctx": nrm((BATCH, CTX_LEN, D_MODEL), 1.0),
        "c_ctx": nrm((D_MODEL,), 1.0),
        "w_mod": nrm((DEPTH, D_MODEL, 6 * D_MODEL), 0.5 * D_MODEL ** -0.5),
        "b_mod": nrm((DEPTH, 6 * D_MODEL), 0.02),
        "ln_g": gain((DEPTH, 2, D_MODEL)),
        "ln_b": nrm((DEPTH, 2, D_MODEL), 0.02),
        "w_in_ab": nrm((N_EVEN, D_MODEL, AB_IN), D_MODEL ** -0.5),
        "w_out_ab": nrm((N_EVEN, AB_OUT, D_MODEL), BETA * AB_OUT ** -0.5),
        "q_gain": gain((N_EVEN, HEAD_DIM)),
        "k_gain": gain((N_EVEN, HEAD_DIM)),
        "w_fourier": nrm((N_EVEN, N_FOURIER_GROUPS, FOURIER_GROUP_DIM, FOURIER_GROUP_DIM), FOURIER_GROUP_DIM ** -0.5),
        "b_fourier": nrm((N_EVEN, N_FOURIER_GROUPS, FOURIER_GROUP_DIM), 0.02),
        "w_in_cd": nrm((N_ODD, D_MODEL, CD_IN), D_MODEL ** -0.5),
        "w_out_cd": nrm((N_ODD, CD_OUT, D_MODEL), BETA * CD_OUT ** -0.5),
        "sgu_g": gain((N_ODD, N_SGU_GROUPS, SGU_GROUP_DIM)),
        "sgu_b": nrm((N_ODD, N_SGU_GROUPS, SGU_GROUP_DIM), 0.02),
        "w_spatial": nrm((N_ODD, N_SGU_GROUPS, CHUNK, CHUNK), CHUNK ** -0.5),
        "b_spatial": gain((N_ODD, N_SGU_GROUPS, CHUNK)),
        "conv_w": nrm((N_ODD, CONV_WIDTH, D_W), CONV_WIDTH ** -0.5),
        "conv_b": nrm((N_ODD, D_W), 0.02),
        "conv_norm_g": gain((N_ODD, N_CONV_GROUPS, CONV_GROUP_DIM)),
        "conv_norm_b": nrm((N_ODD, N_CONV_GROUPS, CONV_GROUP_DIM), 0.02),
        "w_group": nrm((DEPTH, D_MODEL, N_EXPERT_GROUPS), D_MODEL ** -0.5),
        "b_group": nrm((DEPTH, N_EXPERT_GROUPS), 0.01),
        "w_router": nrm((DEPTH, N_EXPERT_GROUPS, D_MODEL, EXPERTS_PER_GROUP), D_MODEL ** -0.5),
        "b_router": nrm((DEPTH, N_EXPERT_GROUPS, EXPERTS_PER_GROUP), 0.01),
        "w_exp_gate": nrm((DEPTH, N_EXPERTS, D_MODEL, D_EXPERT), D_MODEL ** -0.5),
        "w_exp_up": nrm((DEPTH, N_EXPERTS, D_MODEL, D_EXPERT), D_MODEL ** -0.5),
        "w_exp_down": nrm((DEPTH, N_EXPERTS, D_EXPERT, D_MODEL), BETA * D_EXPERT ** -0.5),
    }


def reference(x, c, ctx, c_ctx, w_mod, b_mod, ln_g, ln_b, w_in_ab, w_out_ab, q_gain, k_gain,
              w_fourier, b_fourier, w_in_cd, w_out_cd, sgu_g, sgu_b, w_spatial, b_spatial,
              conv_w, conv_b, conv_norm_g, conv_norm_b, w_group, b_group, w_router, b_router,
              w_exp_gate, w_exp_up, w_exp_down):
    x_lat, x_ctx = x, ctx
    s_c = jax.nn.silu(c)
    s_cc = jax.nn.silu(c_ctx)
    for l in range(DEPTH):
        ctx_out = any(j % 2 == 0 for j in range(l + 1, DEPTH))
        i = l // 2
        sh1, sc1, g1, sh2, sc2, g2 = jnp.split((s_c @ w_mod[l] + b_mod[l])[:, None, :], 6, axis=-1)
        sh1c, sc1c, g1c, sh2c, sc2c, g2c = jnp.split(s_cc @ w_mod[l] + b_mod[l], 6, axis=-1)
        h_lat = x_lat * (1.0 + sc1) + sh1
        if l % 2 == 0:
            h_ctx = x_ctx * (1.0 + sc1c) + sh1c
            y_lat, y_ctx = mixer_ab(h_lat, h_ctx, ctx_out, w_in_ab[i], w_out_ab[i], q_gain[i], k_gain[i],
                                    w_fourier[i], b_fourier[i])
        else:
            cd_params = (w_in_cd[i], w_out_cd[i], sgu_g[i], sgu_b[i], w_spatial[i], b_spatial[i],
                         conv_w[i], conv_b[i], conv_norm_g[i], conv_norm_b[i])
            y_lat = mixer_cd(h_lat, *cd_params)
            if ctx_out:
                y_ctx = mixer_cd(x_ctx * (1.0 + sc1c) + sh1c, *cd_params)
        x_lat = layer_norm(ALPHA * x_lat + g1 * y_lat, ln_g[l, 0], ln_b[l, 0])
        moe_params = (w_group[l], b_group[l], w_router[l], b_router[l], w_exp_gate[l], w_exp_up[l], w_exp_down[l])
        h_lat = (x_lat * (1.0 + sc2) + sh2).reshape(-1, D_MODEL)
        n_lat = h_lat.shape[0]
        if ctx_out:
            x_ctx = layer_norm(ALPHA * x_ctx + g1c * y_ctx, ln_g[l, 0], ln_b[l, 0])
            h_ctx = (x_ctx * (1.0 + sc2c) + sh2c).reshape(-1, D_MODEL)
            y_all = hier_moe(jnp.concatenate([h_lat, h_ctx], axis=0), *moe_params)
            y_lat = y_all[:n_lat].reshape(x_lat.shape)
            x_ctx = layer_norm(ALPHA * x_ctx + g2c * y_all[n_lat:].reshape(x_ctx.shape), ln_g[l, 1], ln_b[l, 1])
        else:
            y_lat = hier_moe(h_lat, *moe_params).reshape(x_lat.shape)
            x_ctx = None
        x_lat = layer_norm(ALPHA * x_lat + g2 * y_lat, ln_g[l, 1], ln_b[l, 1])
    return x_lat
```

```python
import functools

import numpy as np
import jax
import jax.numpy as jnp
from jax import lax
from jax.experimental import pallas as pl
from jax.experimental.pallas import tpu as pltpu

F32 = jnp.float32
BF16 = jnp.bfloat16
HIGHEST = lax.Precision.HIGHEST

LANES = 128
D_MODEL = 1024
DEPTH = 4
GRID_W = 64
HEAD_DIM = 64
N_Q_HEADS = 12
N_KV_HEADS = 4
REP = N_Q_HEADS // N_KV_HEADS
ROPE_THETA = 10000.0
AXIS_DIM = HEAD_DIM // 2
N_FOURIER_GROUPS = 4
FOURIER_GROUP_DIM = 64
CHUNK = 128
N_SGU_GROUPS = 4
SGU_GROUP_DIM = 128
CONV_WIDTH = 31
CONV_HALO = 16
N_CONV_GROUPS = 4
CONV_GROUP_DIM = 128
N_EXPERT_GROUPS = 4
EXPERTS_PER_GROUP = 8
N_EXPERTS = N_EXPERT_GROUPS * EXPERTS_PER_GROUP
D_EXPERT = 256
Q_W = N_Q_HEADS * HEAD_DIM
KV_W = N_KV_HEADS * HEAD_DIM
F_W = N_FOURIER_GROUPS * FOURIER_GROUP_DIM
AB_IN = Q_W + 2 * KV_W + F_W
C_W = N_SGU_GROUPS * SGU_GROUP_DIM
D_W = N_CONV_GROUPS * CONV_GROUP_DIM
CD_IN = 2 * C_W + 2 * D_W
ALPHA = (2 * DEPTH) ** 0.25
EPS = 1e-6
N_Q_TILES = Q_W // LANES
N_KV_TILES = KV_W // LANES
ROUTER_LANES = LANES
VMEM_LIMIT = 56 * 1024 * 1024


def _params(*sem):
    return pltpu.CompilerParams(dimension_semantics=sem, vmem_limit_bytes=VMEM_LIMIT)


def _silu(x):
    return x * jax.nn.sigmoid(x)


def _layer_norm(z, g, b):
    mu = jnp.mean(z, axis=-1, keepdims=True)
    zc = z - mu
    var = jnp.mean(zc * zc, axis=-1, keepdims=True)
    return zc * lax.rsqrt(var + EPS) * g + b


def _mod_kernel(s_ref, w_ref, b_ref, o_ref):
    s = _silu(s_ref[...])
    o_ref[0] = jnp.dot(s, w_ref[0], preferred_element_type=F32, precision=HIGHEST) + b_ref[0]


def _modulation(cond, w_mod, b_mod):
    n_layers, d, six_d = w_mod.shape
    rows = cond.shape[0]
    tn = 1536
    return pl.pallas_call(
        _mod_kernel,
        grid=(n_layers, six_d // tn),
        in_specs=[pl.BlockSpec((rows, d), lambda l, j: (0, 0)),
                  pl.BlockSpec((1, d, tn), lambda l, j: (l, 0, j)),
                  pl.BlockSpec((1, 1, tn), lambda l, j: (l, 0, j))],
        out_specs=pl.BlockSpec((1, rows, tn), lambda l, j: (l, 0, j)),
        out_shape=jax.ShapeDtypeStruct((n_layers, rows, six_d), F32),
        compiler_params=_params("parallel", "parallel"),
        name="modulation",
    )(cond, w_mod, b_mod.reshape(n_layers, 1, six_d))


def _ab_in_kernel(x_ref, sc_ref, sh_ref, w_ref, cos_ref, sin_ref, qg_ref, kg_ref,
                  q_ref, k_ref, v_ref, f_ref):
    tm = x_ref.shape[1]
    h = (x_ref[0] * (1.0 + sc_ref[0]) + sh_ref[0]).astype(BF16)
    p = jnp.dot(h, w_ref[...], preferred_element_type=F32)
    lane = lax.broadcasted_iota(jnp.int32, (tm, LANES), 1)
    low_head = lane < HEAD_DIM
    first_half = (lane % HEAD_DIM) < AXIS_DIM
    cos = cos_ref[...]
    sin = sin_ref[...]

    def norm_rope(t, gain):
        sq = t * t
        ss_lo = jnp.sum(jnp.where(low_head, sq, 0.0), axis=-1, keepdims=True)
        ss_hi = jnp.sum(jnp.where(low_head, 0.0, sq), axis=-1, keepdims=True)
        ms = jnp.where(low_head, ss_lo, ss_hi) * (1.0 / HEAD_DIM)
        tn = t * lax.rsqrt(ms + EPS) * gain
        partner = jnp.where(first_half,
                            pltpu.roll(tn, LANES - AXIS_DIM, 1),
                            pltpu.roll(tn, AXIS_DIM, 1))
        return tn * cos + partner * sin

    for j in range(N_Q_TILES):
        t = norm_rope(p[:, j * LANES:(j + 1) * LANES], qg_ref[...])
        q_ref[0, j] = (t * (HEAD_DIM ** -0.5)).astype(BF16)
    for j in range(N_KV_TILES):
        c0 = Q_W + j * LANES
        k_ref[0, j] = norm_rope(p[:, c0:c0 + LANES], kg_ref[...]).astype(BF16)
        c1 = Q_W + KV_W + j * LANES
        v_ref[0, j] = p[:, c1:c1 + LANES].astype(BF16)
    f_ref[0] = p[:, Q_W + 2 * KV_W:].astype(BF16)


def _ab_in_proj(x, sc, sh, w_bf, cos, sin, qg, kg, tm):
    bsz, n, d = x.shape
    tm = min(tm, n)
    vec = pl.BlockSpec((1, 1, d), lambda b, i: (b, 0, 0))
    tab = pl.BlockSpec((tm, LANES), lambda b, i: (i, 0))
    gain = pl.BlockSpec((1, LANES), lambda b, i: (0, 0))
    return pl.pallas_call(
        _ab_in_kernel,
        grid=(bsz, n // tm),
        in_specs=[pl.BlockSpec((1, tm, d), lambda b, i: (b, i, 0)), vec, vec,
                  pl.BlockSpec((d, AB_IN), lambda b, i: (0, 0)), tab, tab, gain, gain],
        out_specs=[pl.BlockSpec((1, N_Q_TILES, tm, LANES), lambda b, i: (b, 0, i, 0)),
                   pl.BlockSpec((1, N_KV_TILES, tm, LANES), lambda b, i: (b, 0, i, 0)),
                   pl.BlockSpec((1, N_KV_TILES, tm, LANES), lambda b, i: (b, 0, i, 0)),
                   pl.BlockSpec((1, tm, F_W), lambda b, i: (b, i, 0))],
        out_shape=[jax.ShapeDtypeStruct((bsz, N_Q_TILES, n, LANES), BF16),
                   jax.ShapeDtypeStruct((bsz, N_KV_TILES, n, LANES), BF16),
                   jax.ShapeDtypeStruct((bsz, N_KV_TILES, n, LANES), BF16),
                   jax.ShapeDtypeStruct((bsz, n, F_W), BF16)],
        compiler_params=_params("parallel", "parallel"),
        name="ab_in_proj",
    )(x, sc, sh, w_bf, cos, sin, qg, kg)


def _attn_kernel(*refs, n_seg):
    q_ref = refs[0]
    k_refs = refs[1:1 + n_seg]
    v_refs = refs[1 + n_seg:1 + 2 * n_seg]
    o_ref = refs[1 + 2 * n_seg]
    tq = q_ref.shape[2]
    lane = lax.broadcasted_iota(jnp.int32, (tq, LANES), 1)
    low_head = lane < HEAD_DIM
    zero = jnp.zeros((), BF16)
    dn = (((1,), (1,)), ((), ()))
    for j in range(REP):
        q = q_ref[0, j]
        q2 = jnp.concatenate([jnp.where(low_head, q, zero), jnp.where(low_head, zero, q)], axis=0)
        s = [lax.dot_general(q2, k_ref[0, 0], dn, preferred_element_type=F32) for k_ref in k_refs]
        m = s[0].max(axis=-1, keepdims=True)
        for si in s[1:]:
            m = jnp.maximum(m, si.max(axis=-1, keepdims=True))
        p = [jnp.exp(si - m) for si in s]
        den = p[0].sum(axis=-1, keepdims=True)
        for pi in p[1:]:
            den = den + pi.sum(axis=-1, keepdims=True)
        o2 = jnp.dot(p[0].astype(BF16), v_refs[0][0, 0], preferred_element_type=F32)
        for pi, v_ref in zip(p[1:], v_refs[1:]):
            o2 = o2 + jnp.dot(pi.astype(BF16), v_ref[0, 0], preferred_element_type=F32)
        o2 = o2 / den
        o_ref[0, :, j * LANES:(j + 1) * LANES] = jnp.where(low_head, o2[:tq], o2[tq:]).astype(BF16)


def _attention(q, ks, vs, tq):
    bsz, _, n, _ = q.shape
    tq = min(tq, n)
    n_seg = len(ks)
    kv_specs = [pl.BlockSpec((1, 1, a.shape[2], LANES), lambda b, i, t: (b, t, 0, 0)) for a in ks + vs]
    return pl.pallas_call(
        functools.partial(_attn_kernel, n_seg=n_seg),
        grid=(bsz, n // tq, N_KV_TILES),
        in_specs=[pl.BlockSpec((1, REP, tq, LANES), lambda b, i, t: (b, t, i, 0))] + kv_specs,
        out_specs=pl.BlockSpec((1, tq, REP * LANES), lambda b, i, t: (b, i, t)),
        out_shape=jax.ShapeDtypeStruct((bsz, n, Q_W), BF16),
        compiler_params=_params("parallel", "parallel", "arbitrary"),
        name="attention",
    )(q, *ks, *vs)


def _fourier_kernel(f_ref, wc_ref, cs_ref, wf_ref, bf_ref, o_ref, y_ref):
    n = f_ref.shape[1]

    @pl.when(pl.program_id(1) == 0)
    def _():
        y = jnp.dot(f_ref[0], wc_ref[...], preferred_element_type=F32)
        y_ref[0:n, :] = y[:, :F_W].astype(BF16)
        y_ref[n:2 * n, :] = y[:, F_W:].astype(BF16)

    mixed = jnp.dot(cs_ref[...], y_ref[...], preferred_element_type=F32)
    out = jnp.dot(mixed.astype(BF16), wf_ref[...], preferred_element_type=F32) + bf_ref[...]
    o_ref[0] = out.astype(BF16)


def _fourier_mix(f, wc, cs, wf, bf, tr):
    bsz, n, _ = f.shape
    tr = min(tr, n)
    return pl.pallas_call(
        _fourier_kernel,
        grid=(bsz, n // tr),
        in_specs=[pl.BlockSpec((1, n, F_W), lambda b, i: (b, 0, 0)),
                  pl.BlockSpec((F_W, 2 * F_W), lambda b, i: (0, 0)),
                  pl.BlockSpec((tr, 2 * n), lambda b, i: (i, 0)),
                  pl.BlockSpec((F_W, F_W), lambda b, i: (0, 0)),
                  pl.BlockSpec((1, F_W), lambda b, i: (0, 0))],
        out_specs=pl.BlockSpec((1, tr, F_W), lambda b, i: (b, i, 0)),
        out_shape=jax.ShapeDtypeStruct((bsz, n, F_W), BF16),
        scratch_shapes=[pltpu.VMEM((2 * n, F_W), BF16)],
        compiler_params=_params("parallel", "arbitrary"),
        name="fourier_mix",
    )(f, wc, cs, wf, bf)


def _dft_tables(n):
    k = np.arange(n, dtype=np.int64)
    ang = 2.0 * np.pi * ((k[:, None] * k[None, :]) % n).astype(np.float64) / n
    scale = 1.0 / np.sqrt(float(n) * FOURIER_GROUP_DIM)
    cs = np.concatenate([np.cos(ang), -np.sin(ang)], axis=1) * scale
    c = np.arange(FOURIER_GROUP_DIM, dtype=np.int64)
    angc = 2.0 * np.pi * ((c[:, None] * c[None, :]) % FOURIER_GROUP_DIM).astype(np.float64) / FOURIER_GROUP_DIM
    eye = np.eye(N_FOURIER_GROUPS)
    wc = np.concatenate([np.kron(eye, np.cos(angc)), np.kron(eye, np.sin(angc))], axis=1)
    return cs.astype(np.float32), wc.astype(np.float32)


def _route(lg):
    lane = lax.broadcasted_iota(jnp.int32, lg.shape, 1).astype(F32)
    neg = jnp.float32(-1e30)
    first = lambda hit: jnp.min(jnp.where(hit, lane, float(LANES)), axis=-1, keepdims=True)
    gl = jnp.where(lane < N_EXPERT_GROUPS, lg, neg)
    gmax = gl.max(axis=-1, keepdims=True)
    g_idx = first(gl == gmax)
    g_w = 1.0 / jnp.sum(jnp.exp(gl - gmax), axis=-1, keepdims=True)
    lo = N_EXPERT_GROUPS + EXPERTS_PER_GROUP * g_idx
    el = jnp.where((lane >= lo) & (lane < lo + EXPERTS_PER_GROUP), lg, neg)
    v1 = el.max(axis=-1, keepdims=True)
    i1 = first(el == v1)
    el2 = jnp.where(lane == i1, neg, el)
    v2 = el2.max(axis=-1, keepdims=True)
    i2 = first(el2 == v2)
    e2 = jnp.exp(v2 - v1)
    w1 = g_w / (1.0 + e2)
    w2 = g_w * e2 / (1.0 + e2)
    return jnp.where(lane == i1, w1, 0.0) + jnp.where(lane == i2, w2, 0.0)


def _out_ln_route_kernel(a_ref, b_ref, wa_ref, wb_ref, x_ref, g1_ref, sc2_ref, sh2_ref,
                         lng_ref, lnb_ref, wr_ref, br_ref, x1_ref, h2_ref, comb_ref):
    y = jnp.dot(a_ref[0], wa_ref[...], preferred_element_type=F32)
    y = y + jnp.dot(b_ref[0], wb_ref[...], preferred_element_type=F32)
    x1 = _layer_norm(ALPHA * x_ref[0] + g1_ref[0] * y, lng_ref[...], lnb_ref[...])
    x1_ref[0] = x1
    h2 = x1 * (1.0 + sc2_ref[0]) + sh2_ref[0]
    h2_ref[0] = h2.astype(BF16)
    lg = jnp.dot(h2, wr_ref[...], preferred_element_type=F32, precision=HIGHEST) + br_ref[...]
    comb_ref[0] = _route(lg)


def _out_ln_route(a, b, wa, wb, x, g1, sc2, sh2, lng, lnb, wr, br, tm):
    bsz, n, d = x.shape
    tm = min(tm, n)
    ka, kb = a.shape[-1], b.shape[-1]
    vec = pl.BlockSpec((1, 1, d), lambda bb, i: (bb, 0, 0))
    row = pl.BlockSpec((1, d), lambda bb, i: (0, 0))
    tile = lambda w: pl.BlockSpec((1, tm, w), lambda bb, i: (bb, i, 0))
    full = lambda r, c: pl.BlockSpec((r, c), lambda bb, i: (0, 0))
    return pl.pallas_call(
        _out_ln_route_kernel,
        grid=(bsz, n // tm),
        in_specs=[tile(ka), tile(kb), full(ka, d), full(kb, d), tile(d), vec, vec, vec, row, row,
                  full(d, ROUTER_LANES), full(1, ROUTER_LANES)],
        out_specs=[tile(d), tile(d), tile(ROUTER_LANES)],
        out_shape=[jax.ShapeDtypeStruct((bsz, n, d), F32),
                   jax.ShapeDtypeStruct((bsz, n, d), BF16),
                   jax.ShapeDtypeStruct((bsz, n, ROUTER_LANES), F32)],
        compiler_params=_params("parallel", "parallel"),
        name="out_ln_route",
    )(a, b, wa, wb, x, g1, sc2, sh2, lng, lnb, wr, br)


def _moe_kernel(h_ref, comb_ref, wg_ref, wu_ref, wd_ref, x1_ref, g2_ref, lng_ref, lnb_ref,
                o_ref, acc_ref):
    e = pl.program_id(2)

    @pl.when(e == 0)
    def _():
        acc_ref[...] = jnp.zeros_like(acc_ref)

    h = h_ref[0]
    gate = jnp.dot(h, wg_ref[0], preferred_element_type=F32)
    up = jnp.dot(h, wu_ref[0], preferred_element_type=F32)
    hid = (_silu(gate) * up).astype(BF16)
    y = jnp.dot(hid, wd_ref[0], preferred_element_type=F32)
    comb = comb_ref[0]
    lane = lax.broadcasted_iota(jnp.int32, comb.shape, 1)
    cw = jnp.sum(jnp.where(lane == e + N_EXPERT_GROUPS, comb, 0.0), axis=-1, keepdims=True)
    acc_ref[...] += cw * y

    @pl.when(e == N_EXPERTS - 1)
    def _():
        z = ALPHA * x1_ref[0] + g2_ref[0] * acc_ref[...]
        o_ref[0] = _layer_norm(z, lng_ref[...], lnb_ref[...])


def _moe_dense(h2, comb, wg, wu, wd, x1, g2, lng, lnb, tm):
    bsz, n, d = x1.shape
    tm = min(tm, n)
    tile = lambda w: pl.BlockSpec((1, tm, w), lambda bb, i, e: (bb, i, 0))
    return pl.pallas_call(
        _moe_kernel,
        grid=(bsz, n // tm, N_EXPERTS),
        in_specs=[tile(d), tile(ROUTER_LANES),
                  pl.BlockSpec((1, d, D_EXPERT), lambda bb, i, e: (e, 0, 0)),
                  pl.BlockSpec((1, d, D_EXPERT), lambda bb, i, e: (e, 0, 0)),
                  pl.BlockSpec((1, D_EXPERT, d), lambda bb, i, e: (e, 0, 0)),
                  tile(d),
                  pl.BlockSpec((1, 1, d), lambda bb, i, e: (bb, 0, 0)),
                  pl.BlockSpec((1, d), lambda bb, i, e: (0, 0)),
                  pl.BlockSpec((1, d), lambda bb, i, e: (0, 0))],
        out_specs=tile(d),
        out_shape=jax.ShapeDtypeStruct((bsz, n, d), F32),
        scratch_shapes=[pltpu.VMEM((tm, d), F32)],
        compiler_params=_params("parallel", "parallel", "arbitrary"),
        name="moe_dense",
    )(h2, comb, wg, wu, wd, x1, g2, lng, lnb)


def _gelu_tanh(x):
    return 0.5 * x * (1.0 + jnp.tanh(np.sqrt(2.0 / np.pi).astype(np.float32) * (x + 0.044715 * (x * x * x))))


def _cd_in_kernel(x_ref, sc_ref, sh_ref, w_ref, sg_ref, sb_ref, wsp_ref, bsp_ref, yc_ref, glu_ref):
    tm = x_ref.shape[1]
    h = (x_ref[0] * (1.0 + sc_ref[0]) + sh_ref[0]).astype(BF16)
    p = jnp.dot(h, w_ref[...], preferred_element_type=F32)
    for g in range(N_SGU_GROUPS):
        u = _gelu_tanh(p[:, g * SGU_GROUP_DIM:(g + 1) * SGU_GROUP_DIM])
        v = _gelu_tanh(p[:, C_W + g * SGU_GROUP_DIM:C_W + (g + 1) * SGU_GROUP_DIM])
        vg = _layer_norm(v, sg_ref[g:g + 1, :], sb_ref[g:g + 1, :]).astype(BF16)
        for c in range(tm // CHUNK):
            rows = slice(c * CHUNK, (c + 1) * CHUNK)
            sv = jnp.dot(wsp_ref[g], vg[rows], preferred_element_type=F32) + bsp_ref[g]
            yc_ref[0, rows, g * SGU_GROUP_DIM:(g + 1) * SGU_GROUP_DIM] = (u[rows] * sv).astype(BF16)
    a = p[:, 2 * C_W:2 * C_W + D_W]
    gate = p[:, 2 * C_W + D_W:]
    glu_ref[0] = a * jax.nn.sigmoid(gate)


def _cd_in_proj(x, sc, sh, w_bf, sg, sb, wsp_bf, bsp, tm):
    bsz, n, d = x.shape
    tm = min(tm, n)
    vec = pl.BlockSpec((1, 1, d), lambda b, i: (b, 0, 0))
    return pl.pallas_call(
        _cd_in_kernel,
        grid=(bsz, n // tm),
        in_specs=[pl.BlockSpec((1, tm, d), lambda b, i: (b, i, 0)), vec, vec,
                  pl.BlockSpec((d, CD_IN), lambda b, i: (0, 0)),
                  pl.BlockSpec((N_SGU_GROUPS, SGU_GROUP_DIM), lambda b, i: (0, 0)),
                  pl.BlockSpec((N_SGU_GROUPS, SGU_GROUP_DIM), lambda b, i: (0, 0)),
                  pl.BlockSpec((N_SGU_GROUPS, CHUNK, CHUNK), lambda b, i: (0, 0, 0)),
                  pl.BlockSpec((N_SGU_GROUPS, CHUNK, SGU_GROUP_DIM), lambda b, i: (0, 0, 0))],
        out_specs=[pl.BlockSpec((1, tm, C_W), lambda b, i: (b, i, 0)),
                   pl.BlockSpec((1, tm, D_W), lambda b, i: (b, i, 0))],
        out_shape=[jax.ShapeDtypeStruct((bsz, n, C_W), BF16),
                   jax.ShapeDtypeStruct((bsz, n, D_W), F32)],
        compiler_params=_params("parallel", "parallel"),
        name="cd_in_proj",
    )(x, sc, sh, w_bf, sg, sb, wsp_bf, bsp)


def _conv_kernel(glu_ref, w_ref, cb_ref, g_ref, b_ref, o_ref, pad_ref, *, rows):
    n = glu_ref.shape[1]
    halo = jnp.zeros((CONV_HALO, CONV_GROUP_DIM), F32)
    pad_ref[0:CONV_HALO, :] = halo
    pad_ref[CONV_HALO + n:CONV_HALO + n + CONV_HALO, :] = halo
    pad_ref[CONV_HALO:CONV_HALO + n, :] = glu_ref[0]
    first = CONV_HALO - CONV_WIDTH // 2
    for c in range(n // rows):
        acc = jnp.zeros((rows, CONV_GROUP_DIM), F32) + cb_ref[...]
        for k in range(CONV_WIDTH):
            start = c * rows + first + k
            acc = acc + w_ref[k:k + 1, :] * pad_ref[start:start + rows, :]
        dn = _layer_norm(acc, g_ref[...], b_ref[...])
        o_ref[0, c * rows:(c + 1) * rows, :] = _silu(dn).astype(BF16)


def _conv_module(glu, conv_w, conv_b, cn_g, cn_b, rows):
    bsz, n, _ = glu.shape
    rows = min(rows, n)
    vec = pl.BlockSpec((1, CONV_GROUP_DIM), lambda b, g: (0, g))
    return pl.pallas_call(
        functools.partial(_conv_kernel, rows=rows),
        grid=(bsz, N_CONV_GROUPS),
        in_specs=[pl.BlockSpec((1, n, CONV_GROUP_DIM), lambda b, g: (b, 0, g)),
                  pl.BlockSpec((CONV_WIDTH, CONV_GROUP_DIM), lambda b, g: (0, g)),
                  vec, vec, vec],
        out_specs=pl.BlockSpec((1, n, CONV_GROUP_DIM), lambda b, g: (b, 0, g)),
        out_shape=jax.ShapeDtypeStruct((bsz, n, D_W), BF16),
        scratch_shapes=[pltpu.VMEM((n + 2 * CONV_HALO, CONV_GROUP_DIM), F32)],
        compiler_params=_params("parallel", "parallel"),
        name="conv_module",
    )(glu, conv_w, conv_b, cn_g, cn_b)


def _head_pair_columns():
    inner = np.concatenate([np.arange(0, HEAD_DIM, 2), np.arange(1, HEAD_DIM, 2)])
    cols = []
    for t in range(N_KV_TILES):
        for r in range(REP):
            for g in (2 * t, 2 * t + 1):
                cols.append((g * REP + r) * HEAD_DIM + inner)
    for g in range(N_KV_HEADS):
        cols.append(Q_W + g * HEAD_DIM + inner)
    cols.append(np.arange(Q_W + KV_W, AB_IN))
    return np.concatenate(cols)


def _attn_out_rows():
    rows = []
    for t in range(N_KV_TILES):
        for r in range(REP):
            for g in (2 * t, 2 * t + 1):
                rows.append((g * REP + r) * HEAD_DIM + np.arange(HEAD_DIM))
    return np.concatenate(rows)


def _rope_tables(n, rotate):
    if not rotate:
        return jnp.ones((n, LANES), F32), jnp.zeros((n, LANES), F32)
    rows = n // GRID_W
    row = jnp.repeat(jnp.arange(rows, dtype=F32), GRID_W)
    col = jnp.tile(jnp.arange(GRID_W, dtype=F32), rows)
    inv_freq = 1.0 / (ROPE_THETA ** (jnp.arange(0, AXIS_DIM, 2, dtype=F32) / AXIS_DIM))
    ang = jnp.concatenate([row[:, None] * inv_freq, col[:, None] * inv_freq], axis=-1)
    cos, sin = jnp.cos(ang), jnp.sin(ang)
    return jnp.tile(cos, (1, 4)), jnp.tile(jnp.concatenate([-sin, sin], axis=-1), (1, 2))


def _pair_gain(g):
    perm = np.concatenate([np.arange(0, HEAD_DIM, 2), np.arange(1, HEAD_DIM, 2)])
    return jnp.tile(g[perm], 2).reshape(1, LANES)


def _block_diag(w):
    g, a, b = w.shape
    out = jnp.zeros((g * a, g * b), w.dtype)
    for i in range(g):
        out = out.at[i * a:(i + 1) * a, i * b:(i + 1) * b].set(w[i])
    return out


def _router_weights(w_group, b_group, w_router, b_router):
    d = w_group.shape[0]
    wr = jnp.concatenate([w_group, jnp.transpose(w_router, (1, 0, 2)).reshape(d, N_EXPERTS)], axis=1)
    br = jnp.concatenate([b_group, b_router.reshape(N_EXPERTS)])
    pad = ROUTER_LANES - wr.shape[1]
    return jnp.pad(wr, ((0, 0), (0, pad))), jnp.pad(br, (0, pad)).reshape(1, ROUTER_LANES)


def _vec(mod_l, k, rows):
    return mod_l[rows, k * D_MODEL:(k + 1) * D_MODEL][:, None, :]


def _channel_mix(y_a, y_b, wa, wb, x, mod_l, rows, ln_g, ln_b, router, experts, tm):
    lng = lambda i: ln_g[i].reshape(1, D_MODEL)
    lnb = lambda i: ln_b[i].reshape(1, D_MODEL)
    x1, h2, comb = _out_ln_route(y_a, y_b, wa, wb, x, _vec(mod_l, 2, rows), _vec(mod_l, 4, rows),
                                 _vec(mod_l, 3, rows), lng(0), lnb(0), *router, tm=tm)
    return _moe_dense(h2, comb, *experts, x1, _vec(mod_l, 5, rows), lng(1), lnb(1), tm=512)


def kernel(x, c, ctx, c_ctx, w_mod, b_mod, ln_g, ln_b, w_in_ab, w_out_ab, q_gain, k_gain, w_fourier, b_fourier, w_in_cd, w_out_cd, sgu_g, sgu_b, w_spatial, b_spatial, conv_w, conv_b, conv_norm_g, conv_norm_b, w_group, b_group, w_router, b_router, w_exp_gate, w_exp_up, w_exp_down):
    bsz, n, _ = x.shape
    lc = ctx.shape[1]
    depth = w_mod.shape[0]
    pad_rows = (-(bsz + 1)) % 8
    cond = jnp.concatenate([c, c_ctx[None, :], jnp.zeros((pad_rows, D_MODEL), F32)], axis=0)
    mod = _modulation(cond, w_mod, b_mod)
    lat_rows = np.arange(bsz)
    ctx_rows = np.full((bsz,), bsz)

    in_cols = _head_pair_columns()
    out_rows = _attn_out_rows()
    rope_lat = _rope_tables(n, True)
    rope_ctx = _rope_tables(lc, False)
    dft = {m: _dft_tables(m) for m in {n, lc}}

    x_lat, x_ctx = x, ctx
    for l in range(depth):
        ctx_out = any(j % 2 == 0 for j in range(l + 1, depth))
        i = l // 2
        mod_l = mod[l]
        router = _router_weights(w_group[l], b_group[l], w_router[l], b_router[l])
        experts = (w_exp_gate[l].astype(BF16), w_exp_up[l].astype(BF16), w_exp_down[l].astype(BF16))
        if l % 2 == 0:
            w_in = w_in_ab[i][:, in_cols].astype(BF16)
            w_out = w_out_ab[i].astype(BF16)
            wa, wb = w_out[out_rows], w_out[Q_W:]
            qg, kg = _pair_gain(q_gain[i]), _pair_gain(k_gain[i])
            wf = _block_diag(w_fourier[i]).astype(BF16)
            bf = b_fourier[i].reshape(1, F_W)
            q, k, v, f = _ab_in_proj(x_lat, _vec(mod_l, 1, lat_rows), _vec(mod_l, 0, lat_rows),
                                     w_in, *rope_lat, qg, kg, tm=512)
            qc, kc, vc, fc = _ab_in_proj(x_ctx, _vec(mod_l, 1, ctx_rows), _vec(mod_l, 0, ctx_rows),
                                         w_in, *rope_ctx, qg, kg, tm=256)
            a_lat = _attention(q, [kc, k], [vc, v], tq=256)
            cs, wc = dft[n]
            fm_lat = _fourier_mix(f, jnp.asarray(wc).astype(BF16), jnp.asarray(cs).astype(BF16), wf, bf, tr=512)
            ya_lat, yb_lat = a_lat, fm_lat
            if ctx_out:
                ya_ctx = _attention(qc, [kc], [vc], tq=256)
                cs, wc = dft[lc]
                yb_ctx = _fourier_mix(fc, jnp.asarray(wc).astype(BF16), jnp.asarray(cs).astype(BF16), wf, bf, tr=256)
        else:
            w_in = w_in_cd[i].astype(BF16)
            w_out = w_out_cd[i].astype(BF16)
            wa, wb = w_out[:C_W], w_out[C_W:]
            wsp = w_spatial[i].astype(BF16)
            bsp = jnp.broadcast_to(b_spatial[i][:, :, None], (N_SGU_GROUPS, CHUNK, SGU_GROUP_DIM))
            cd = (conv_w[i], conv_b[i].reshape(1, D_W), conv_norm_g[i].reshape(1, D_W), conv_norm_b[i].reshape(1, D_W))
            ya_lat, glu = _cd_in_proj(x_lat, _vec(mod_l, 1, lat_rows), _vec(mod_l, 0, lat_rows),
                                      w_in, sgu_g[i], sgu_b[i], wsp, bsp, tm=256)
            yb_lat = _conv_module(glu, *cd, rows=256)
            if ctx_out:
                ya_ctx, glu_c = _cd_in_proj(x_ctx, _vec(mod_l, 1, ctx_rows), _vec(mod_l, 0, ctx_rows),
                                            w_in, sgu_g[i], sgu_b[i], wsp, bsp, tm=256)
                yb_ctx = _conv_module(glu_c, *cd, rows=256)
        x_lat = _channel_mix(ya_lat, yb_lat, wa, wb, x_lat, mod_l, lat_rows, ln_g[l], ln_b[l], router, experts, tm=256)
        if ctx_out:
            x_ctx = _channel_mix(ya_ctx, yb_ctx, wa, wb, x_ctx, mod_l, ctx_rows, ln_g[l], ln_b[l], router, experts, tm=256)
        else:
            x_ctx = None
    return x_lat
```

```python
import functools

import numpy as np
import jax
import jax.numpy as jnp
from jax import lax
from jax.experimental import pallas as pl
from jax.experimental.pallas import tpu as pltpu

F32 = jnp.float32
BF16 = jnp.bfloat16
HIGHEST = lax.Precision.HIGHEST

LANES = 128
D_MODEL = 1024
DEPTH = 4
GRID_W = 64
HEAD_DIM = 64
N_Q_HEADS = 12
N_KV_HEADS = 4
REP = N_Q_HEADS // N_KV_HEADS
ROPE_THETA = 10000.0
AXIS_DIM = HEAD_DIM // 2
N_FOURIER_GROUPS = 4
FOURIER_GROUP_DIM = 64
CHUNK = 128
N_SGU_GROUPS = 4
SGU_GROUP_DIM = 128
CONV_WIDTH = 31
CONV_HALO = 16
N_CONV_GROUPS = 4
CONV_GROUP_DIM = 128
N_EXPERT_GROUPS = 4
EXPERTS_PER_GROUP = 8
N_EXPERTS = N_EXPERT_GROUPS * EXPERTS_PER_GROUP
D_EXPERT = 256
Q_W = N_Q_HEADS * HEAD_DIM
KV_W = N_KV_HEADS * HEAD_DIM
F_W = N_FOURIER_GROUPS * FOURIER_GROUP_DIM
AB_IN = Q_W + 2 * KV_W + F_W
C_W = N_SGU_GROUPS * SGU_GROUP_DIM
D_W = N_CONV_GROUPS * CONV_GROUP_DIM
CD_IN = 2 * C_W + 2 * D_W
ALPHA = (2 * DEPTH) ** 0.25
EPS = 1e-6
N_Q_TILES = Q_W // LANES
N_KV_TILES = KV_W // LANES
ROUTER_LANES = LANES
TOP_K = 2
ROUTE_ID = 0
ROUTE_W = ROUTE_ID + TOP_K
EXPERT_TILE = 256
MOE_TOKEN_TILE = 256
ISSUE_UNROLL = 8
VMEM_LIMIT = 56 * 1024 * 1024


def _params(*sem):
    return pltpu.CompilerParams(dimension_semantics=sem, vmem_limit_bytes=VMEM_LIMIT)


def _silu(x):
    return x * jax.nn.sigmoid(x)


def _layer_norm(z, g, b):
    mu = jnp.mean(z, axis=-1, keepdims=True)
    zc = z - mu
    var = jnp.mean(zc * zc, axis=-1, keepdims=True)
    return zc * lax.rsqrt(var + EPS) * g + b


def _mod_kernel(s_ref, w_ref, b_ref, o_ref):
    s = _silu(s_ref[...])
    o_ref[0] = jnp.dot(s, w_ref[0], preferred_element_type=F32, precision=HIGHEST) + b_ref[0]


def _modulation(cond, w_mod, b_mod):
    n_layers, d, six_d = w_mod.shape
    rows = cond.shape[0]
    tn = 1536
    return pl.pallas_call(
        _mod_kernel,
        grid=(n_layers, six_d // tn),
        in_specs=[pl.BlockSpec((rows, d), lambda l, j: (0, 0)),
                  pl.BlockSpec((1, d, tn), lambda l, j: (l, 0, j)),
                  pl.BlockSpec((1, 1, tn), lambda l, j: (l, 0, j))],
        out_specs=pl.BlockSpec((1, rows, tn), lambda l, j: (l, 0, j)),
        out_shape=jax.ShapeDtypeStruct((n_layers, rows, six_d), F32),
        compiler_params=_params("parallel", "parallel"),
        name="modulation",
    )(cond, w_mod, b_mod.reshape(n_layers, 1, six_d))


def _ab_in_kernel(x_ref, sc_ref, sh_ref, w_ref, cos_ref, sin_ref, qg_ref, kg_ref,
                  q_ref, k_ref, v_ref, f_ref):
    tm = x_ref.shape[1]
    h = (x_ref[0] * (1.0 + sc_ref[0]) + sh_ref[0]).astype(BF16)
    p = jnp.dot(h, w_ref[...], preferred_element_type=F32)
    lane = lax.broadcasted_iota(jnp.int32, (tm, LANES), 1)
    low_head = lane < HEAD_DIM
    first_half = (lane % HEAD_DIM) < AXIS_DIM
    cos = cos_ref[...]
    sin = sin_ref[...]

    def norm_rope(t, gain):
        sq = t * t
        ss_lo = jnp.sum(jnp.where(low_head, sq, 0.0), axis=-1, keepdims=True)
        ss_hi = jnp.sum(jnp.where(low_head, 0.0, sq), axis=-1, keepdims=True)
        ms = jnp.where(low_head, ss_lo, ss_hi) * (1.0 / HEAD_DIM)
        tn = t * lax.rsqrt(ms + EPS) * gain
        partner = jnp.where(first_half,
                            pltpu.roll(tn, LANES - AXIS_DIM, 1),
                            pltpu.roll(tn, AXIS_DIM, 1))
        return tn * cos + partner * sin

    for j in range(N_Q_TILES):
        t = norm_rope(p[:, j * LANES:(j + 1) * LANES], qg_ref[...])
        q_ref[0, j] = (t * (HEAD_DIM ** -0.5)).astype(BF16)
    for j in range(N_KV_TILES):
        c0 = Q_W + j * LANES
        k_ref[0, j] = norm_rope(p[:, c0:c0 + LANES], kg_ref[...]).astype(BF16)
        c1 = Q_W + KV_W + j * LANES
        v_ref[0, j] = p[:, c1:c1 + LANES].astype(BF16)
    f_ref[0] = p[:, Q_W + 2 * KV_W:].astype(BF16)


def _ab_in_proj(x, sc, sh, w_bf, cos, sin, qg, kg, tm):
    bsz, n, d = x.shape
    tm = min(tm, n)
    vec = pl.BlockSpec((1, 1, d), lambda b, i: (b, 0, 0))
    tab = pl.BlockSpec((tm, LANES), lambda b, i: (i, 0))
    gain = pl.BlockSpec((1, LANES), lambda b, i: (0, 0))
    return pl.pallas_call(
        _ab_in_kernel,
        grid=(bsz, n // tm),
        in_specs=[pl.BlockSpec((1, tm, d), lambda b, i: (b, i, 0)), vec, vec,
                  pl.BlockSpec((d, AB_IN), lambda b, i: (0, 0)), tab, tab, gain, gain],
        out_specs=[pl.BlockSpec((1, N_Q_TILES, tm, LANES), lambda b, i: (b, 0, i, 0)),
                   pl.BlockSpec((1, N_KV_TILES, tm, LANES), lambda b, i: (b, 0, i, 0)),
                   pl.BlockSpec((1, N_KV_TILES, tm, LANES), lambda b, i: (b, 0, i, 0)),
                   pl.BlockSpec((1, tm, F_W), lambda b, i: (b, i, 0))],
        out_shape=[jax.ShapeDtypeStruct((bsz, N_Q_TILES, n, LANES), BF16),
                   jax.ShapeDtypeStruct((bsz, N_KV_TILES, n, LANES), BF16),
                   jax.ShapeDtypeStruct((bsz, N_KV_TILES, n, LANES), BF16),
                   jax.ShapeDtypeStruct((bsz, n, F_W), BF16)],
        compiler_params=_params("parallel", "parallel"),
        name="ab_in_proj",
    )(x, sc, sh, w_bf, cos, sin, qg, kg)


def _attn_kernel(*refs, n_seg):
    q_ref = refs[0]
    k_refs = refs[1:1 + n_seg]
    v_refs = refs[1 + n_seg:1 + 2 * n_seg]
    o_ref = refs[1 + 2 * n_seg]
    tq = q_ref.shape[2]
    lane = lax.broadcasted_iota(jnp.int32, (tq, LANES), 1)
    low_head = lane < HEAD_DIM
    zero = jnp.zeros((), BF16)
    dn = (((1,), (1,)), ((), ()))
    for j in range(REP):
        q = q_ref[0, j]
        q2 = jnp.concatenate([jnp.where(low_head, q, zero), jnp.where(low_head, zero, q)], axis=0)
        s = [lax.dot_general(q2, k_ref[0, 0], dn, preferred_element_type=F32) for k_ref in k_refs]
        m = s[0].max(axis=-1, keepdims=True)
        for si in s[1:]:
            m = jnp.maximum(m, si.max(axis=-1, keepdims=True))
        p = [jnp.exp(si - m) for si in s]
        den = p[0].sum(axis=-1, keepdims=True)
        for pi in p[1:]:
            den = den + pi.sum(axis=-1, keepdims=True)
        o2 = jnp.dot(p[0].astype(BF16), v_refs[0][0, 0], preferred_element_type=F32)
        for pi, v_ref in zip(p[1:], v_refs[1:]):
            o2 = o2 + jnp.dot(pi.astype(BF16), v_ref[0, 0], preferred_element_type=F32)
        o2 = o2 / den
        o_ref[0, :, j * LANES:(j + 1) * LANES] = jnp.where(low_head, o2[:tq], o2[tq:]).astype(BF16)


def _attention(q, ks, vs, tq):
    bsz, _, n, _ = q.shape
    tq = min(tq, n)
    n_seg = len(ks)
    kv_specs = [pl.BlockSpec((1, 1, a.shape[2], LANES), lambda b, i, t: (b, t, 0, 0)) for a in ks + vs]
    return pl.pallas_call(
        functools.partial(_attn_kernel, n_seg=n_seg),
        grid=(bsz, n // tq, N_KV_TILES),
        in_specs=[pl.BlockSpec((1, REP, tq, LANES), lambda b, i, t: (b, t, i, 0))] + kv_specs,
        out_specs=pl.BlockSpec((1, tq, REP * LANES), lambda b, i, t: (b, i, t)),
        out_shape=jax.ShapeDtypeStruct((bsz, n, Q_W), BF16),
        compiler_params=_params("parallel", "parallel", "arbitrary"),
        name="attention",
    )(q, *ks, *vs)


def _fourier_kernel(f_ref, wc_ref, cs_ref, wf_ref, bf_ref, o_ref, y_ref):
    n = f_ref.shape[1]

    @pl.when(pl.program_id(1) == 0)
    def _():
        y = jnp.dot(f_ref[0], wc_ref[...], preferred_element_type=F32)
        y_ref[0:n, :] = y[:, :F_W].astype(BF16)
        y_ref[n:2 * n, :] = y[:, F_W:].astype(BF16)

    mixed = jnp.dot(cs_ref[...], y_ref[...], preferred_element_type=F32)
    out = jnp.dot(mixed.astype(BF16), wf_ref[...], preferred_element_type=F32) + bf_ref[...]
    o_ref[0] = out.astype(BF16)


def _fourier_mix(f, wc, cs, wf, bf, tr):
    bsz, n, _ = f.shape
    tr = min(tr, n)
    return pl.pallas_call(
        _fourier_kernel,
        grid=(bsz, n // tr),
        in_specs=[pl.BlockSpec((1, n, F_W), lambda b, i: (b, 0, 0)),
                  pl.BlockSpec((F_W, 2 * F_W), lambda b, i: (0, 0)),
                  pl.BlockSpec((tr, 2 * n), lambda b, i: (i, 0)),
                  pl.BlockSpec((F_W, F_W), lambda b, i: (0, 0)),
                  pl.BlockSpec((1, F_W), lambda b, i: (0, 0))],
        out_specs=pl.BlockSpec((1, tr, F_W), lambda b, i: (b, i, 0)),
        out_shape=jax.ShapeDtypeStruct((bsz, n, F_W), BF16),
        scratch_shapes=[pltpu.VMEM((2 * n, F_W), BF16)],
        compiler_params=_params("parallel", "arbitrary"),
        name="fourier_mix",
    )(f, wc, cs, wf, bf)


def _dft_tables(n):
    k = np.arange(n, dtype=np.int64)
    ang = 2.0 * np.pi * ((k[:, None] * k[None, :]) % n).astype(np.float64) / n
    scale = 1.0 / np.sqrt(float(n) * FOURIER_GROUP_DIM)
    cs = np.concatenate([np.cos(ang), -np.sin(ang)], axis=1) * scale
    c = np.arange(FOURIER_GROUP_DIM, dtype=np.int64)
    angc = 2.0 * np.pi * ((c[:, None] * c[None, :]) % FOURIER_GROUP_DIM).astype(np.float64) / FOURIER_GROUP_DIM
    eye = np.eye(N_FOURIER_GROUPS)
    wc = np.concatenate([np.kron(eye, np.cos(angc)), np.kron(eye, np.sin(angc))], axis=1)
    return cs.astype(np.float32), wc.astype(np.float32)


def _route(lg):
    lane = lax.broadcasted_iota(jnp.int32, lg.shape, 1).astype(F32)
    neg = jnp.float32(-1e30)
    first = lambda hit: jnp.min(jnp.where(hit, lane, float(LANES)), axis=-1, keepdims=True)
    gl = jnp.where(lane < N_EXPERT_GROUPS, lg, neg)
    gmax = gl.max(axis=-1, keepdims=True)
    g_idx = first(gl == gmax)
    g_w = 1.0 / jnp.sum(jnp.exp(gl - gmax), axis=-1, keepdims=True)
    lo = N_EXPERT_GROUPS + EXPERTS_PER_GROUP * g_idx
    el = jnp.where((lane >= lo) & (lane < lo + EXPERTS_PER_GROUP), lg, neg)
    v1 = el.max(axis=-1, keepdims=True)
    i1 = first(el == v1)
    el2 = jnp.where(lane == i1, neg, el)
    v2 = el2.max(axis=-1, keepdims=True)
    i2 = first(el2 == v2)
    e2 = jnp.exp(v2 - v1)
    w1 = g_w / (1.0 + e2)
    w2 = g_w * e2 / (1.0 + e2)
    out = jnp.where(lane == ROUTE_ID, i1 - N_EXPERT_GROUPS, 0.0)
    out = out + jnp.where(lane == ROUTE_ID + 1, i2 - N_EXPERT_GROUPS, 0.0)
    return out + jnp.where(lane == ROUTE_W, w1, 0.0) + jnp.where(lane == ROUTE_W + 1, w2, 0.0)


def _out_ln_route_kernel(a_ref, b_ref, wa_ref, wb_ref, x_ref, g1_ref, sc2_ref, sh2_ref,
                         lng_ref, lnb_ref, wr_ref, br_ref, x1_ref, h2_ref, route_ref):
    y = jnp.dot(a_ref[0], wa_ref[...], preferred_element_type=F32)
    y = y + jnp.dot(b_ref[0], wb_ref[...], preferred_element_type=F32)
    x1 = _layer_norm(ALPHA * x_ref[0] + g1_ref[0] * y, lng_ref[...], lnb_ref[...])
    x1_ref[0] = x1
    h2 = x1 * (1.0 + sc2_ref[0]) + sh2_ref[0]
    h2_ref[0] = h2
    lg = jnp.dot(h2, wr_ref[...], preferred_element_type=F32, precision=HIGHEST) + br_ref[...]
    route_ref[0] = _route(lg)


def _out_ln_route(a, b, wa, wb, x, g1, sc2, sh2, lng, lnb, wr, br, tm):
    bsz, n, d = x.shape
    tm = min(tm, n)
    ka, kb = a.shape[-1], b.shape[-1]
    vec = pl.BlockSpec((1, 1, d), lambda bb, i: (bb, 0, 0))
    row = pl.BlockSpec((1, d), lambda bb, i: (0, 0))
    tile = lambda w: pl.BlockSpec((1, tm, w), lambda bb, i: (bb, i, 0))
    full = lambda r, c: pl.BlockSpec((r, c), lambda bb, i: (0, 0))
    return pl.pallas_call(
        _out_ln_route_kernel,
        grid=(bsz, n // tm),
        in_specs=[tile(ka), tile(kb), full(ka, d), full(kb, d), tile(d), vec, vec, vec, row, row,
                  full(d, ROUTER_LANES), full(1, ROUTER_LANES)],
        out_specs=[tile(d), tile(d), tile(ROUTER_LANES)],
        out_shape=[jax.ShapeDtypeStruct((bsz, n, d), F32),
                   jax.ShapeDtypeStruct((bsz, n, d), F32),
                   jax.ShapeDtypeStruct((bsz, n, ROUTER_LANES), F32)],
        compiler_params=_params("parallel", "parallel"),
        name="out_ln_route",
    )(a, b, wa, wb, x, g1, sc2, sh2, lng, lnb, wr, br)


def _moe_plan(route, n_tiles):
    ids = route[:, ROUTE_ID:ROUTE_ID + TOP_K].astype(jnp.int32)
    flat = ids.T.reshape(-1)
    onehot = (flat[:, None] == jnp.arange(N_EXPERTS, dtype=jnp.int32)[None, :]).astype(jnp.int32)
    csum = jnp.cumsum(onehot, axis=0)
    counts = csum[-1]
    padded = ((counts + EXPERT_TILE - 1) // EXPERT_TILE) * EXPERT_TILE
    ends = jnp.cumsum(padded)
    starts = ends - padded
    pos = jnp.sum((csum - onehot + starts[None, :]) * onehot, axis=1)
    tile_start = jnp.arange(n_tiles, dtype=jnp.int32) * EXPERT_TILE
    tile_expert = jnp.minimum(jnp.sum((tile_start[:, None] >= ends[None, :]).astype(jnp.int32), axis=1),
                              N_EXPERTS - 1)
    used = (ends[-1] // EXPERT_TILE).reshape(1)
    return pos.reshape(TOP_K, -1), ends, tile_expert, used


def _row_copy(src, src_row, dst, dst_row, sem):
    return pltpu.make_async_copy(src.at[pl.ds(src_row, 1), :], dst.at[pl.ds(dst_row, 1), :], sem)


def _dispatch_kernel(ends_ref, pos_ref, h_ref, x_hbm, zero_ref, sem, zsem):
    tm = h_ref.shape[0]

    @pl.when(pl.program_id(0) == 0)
    def _():
        zero_ref[...] = jnp.zeros_like(zero_ref)
        for e in range(N_EXPERTS):
            end = ends_ref[e]
            begin = ends_ref[e - 1] if e else 0

            @pl.when(end > begin)
            def _():
                start = pl.multiple_of(end - EXPERT_TILE, EXPERT_TILE)
                cp = pltpu.make_async_copy(zero_ref, x_hbm.at[pl.ds(start, EXPERT_TILE), :], zsem)
                cp.start()
                cp.wait()

    def issue(i, carry):
        for u in range(ISSUE_UNROLL):
            r = i * ISSUE_UNROLL + u
            for k in range(TOP_K):
                _row_copy(h_ref, r, x_hbm, pos_ref[0, 0, k * tm + r], sem).start()
        return carry

    def drain(i, carry):
        for _ in range(ISSUE_UNROLL * TOP_K):
            _row_copy(h_ref, 0, x_hbm, 0, sem).wait()
        return carry

    lax.fori_loop(0, tm // ISSUE_UNROLL, issue, 0)
    lax.fori_loop(0, tm // ISSUE_UNROLL, drain, 0)


def _dispatch(h2, pos, ends, n_rows, tm):
    t, d = h2.shape
    pos_blocks = pos.reshape(TOP_K, t // tm, tm).transpose(1, 0, 2).reshape(t // tm, 1, TOP_K * tm)
    return pl.pallas_call(
        _dispatch_kernel,
        grid_spec=pltpu.PrefetchScalarGridSpec(
            num_scalar_prefetch=1,
            grid=(t // tm,),
            in_specs=[pl.BlockSpec((1, 1, TOP_K * tm), lambda i, ends: (i, 0, 0), memory_space=pltpu.SMEM),
                      pl.BlockSpec((tm, d), lambda i, ends: (i, 0))],
            out_specs=pl.BlockSpec(memory_space=pl.ANY),
            scratch_shapes=[pltpu.VMEM((EXPERT_TILE, d), F32),
                            pltpu.SemaphoreType.DMA, pltpu.SemaphoreType.DMA]),
        out_shape=jax.ShapeDtypeStruct((n_rows, d), F32),
        compiler_params=_params("arbitrary"),
        name="moe_dispatch",
    )(ends, pos_blocks, h2)


def _expert_kernel(te_ref, used_ref, x_ref, wg_ref, wu_ref, wd_ref, y_ref):
    j = pl.program_id(0)

    @pl.when(j < used_ref[0])
    def _():
        x = x_ref[...].astype(BF16)
        gate = jnp.dot(x, wg_ref[0], preferred_element_type=F32)
        up = jnp.dot(x, wu_ref[0], preferred_element_type=F32)
        hid = (_silu(gate) * up).astype(BF16)
        y_ref[...] = jnp.dot(hid, wd_ref[0], preferred_element_type=F32)

    @pl.when(j >= used_ref[0])
    def _():
        y_ref[...] = jnp.zeros_like(y_ref)


def _experts(xs, tile_expert, used, wg, wu, wd):
    n_rows, d = xs.shape
    n_tiles = n_rows // EXPERT_TILE
    by_expert = lambda j, te, used: (te[j], 0, 0)
    return pl.pallas_call(
        _expert_kernel,
        grid_spec=pltpu.PrefetchScalarGridSpec(
            num_scalar_prefetch=2,
            grid=(n_tiles,),
            in_specs=[pl.BlockSpec((EXPERT_TILE, d), lambda j, te, used: (jnp.minimum(j, used[0] - 1), 0)),
                      pl.BlockSpec((1, d, D_EXPERT), by_expert),
                      pl.BlockSpec((1, d, D_EXPERT), by_expert),
                      pl.BlockSpec((1, D_EXPERT, d), by_expert)],
            out_specs=pl.BlockSpec((EXPERT_TILE, d), lambda j, te, used: (j, 0))),
        out_shape=jax.ShapeDtypeStruct((n_rows, d), F32),
        compiler_params=_params("arbitrary"),
        name="moe_experts",
    )(tile_expert, used, xs, wg, wu, wd)


def _combine_kernel(pos_ref, route_ref, x1_ref, g2_ref, lng_ref, lnb_ref, y_hbm, o_ref, ybuf, sem):
    tm = x1_ref.shape[0]

    def issue(i, carry):
        for u in range(ISSUE_UNROLL):
            r = i * ISSUE_UNROLL + u
            for k in range(TOP_K):
                _row_copy(y_hbm, pos_ref[0, 0, k * tm + r], ybuf.at[k], r, sem).start()
        return carry

    def drain(i, carry):
        for _ in range(ISSUE_UNROLL * TOP_K):
            _row_copy(y_hbm, 0, ybuf.at[0], 0, sem).wait()
        return carry

    lax.fori_loop(0, tm // ISSUE_UNROLL, issue, 0)
    lax.fori_loop(0, tm // ISSUE_UNROLL, drain, 0)

    route = route_ref[...]
    lane = lax.broadcasted_iota(jnp.int32, route.shape, 1)
    y = jnp.zeros_like(o_ref)
    for k in range(TOP_K):
        w = jnp.sum(jnp.where(lane == ROUTE_W + k, route, 0.0), axis=-1, keepdims=True)
        y = y + w * ybuf[k]
    z = ALPHA * x1_ref[...] + g2_ref[0] * y
    o_ref[...] = _layer_norm(z, lng_ref[...], lnb_ref[...])


def _combine(y, pos, route, x1, g2, lng, lnb, tiles_per_batch, tm):
    t, d = x1.shape
    pos_blocks = pos.reshape(TOP_K, t // tm, tm).transpose(1, 0, 2).reshape(t // tm, 1, TOP_K * tm)
    return pl.pallas_call(
        _combine_kernel,
        grid=(t // tm,),
        in_specs=[pl.BlockSpec((1, 1, TOP_K * tm), lambda i: (i, 0, 0), memory_space=pltpu.SMEM),
                  pl.BlockSpec((tm, ROUTER_LANES), lambda i: (i, 0)),
                  pl.BlockSpec((tm, d), lambda i: (i, 0)),
                  pl.BlockSpec((1, 1, d), lambda i: (i // tiles_per_batch, 0, 0)),
                  pl.BlockSpec((1, d), lambda i: (0, 0)),
                  pl.BlockSpec((1, d), lambda i: (0, 0)),
                  pl.BlockSpec(memory_space=pl.ANY)],
        out_specs=pl.BlockSpec((tm, d), lambda i: (i, 0)),
        out_shape=jax.ShapeDtypeStruct((t, d), F32),
        scratch_shapes=[pltpu.VMEM((TOP_K, tm, d), F32), pltpu.SemaphoreType.DMA],
        compiler_params=_params("arbitrary"),
        name="moe_combine",
    )(pos_blocks, route, x1, g2, lng, lnb, y)


def _moe(h2, route, experts, x1, g2, lng, lnb):
    bsz, n, d = x1.shape
    t = bsz * n
    tm = min(MOE_TOKEN_TILE, n)
    n_tiles = TOP_K * t // EXPERT_TILE + N_EXPERTS
    route = route.reshape(t, ROUTER_LANES)
    pos, ends, tile_expert, used = _moe_plan(route, n_tiles)
    xs = _dispatch(h2.reshape(t, d), pos, ends, n_tiles * EXPERT_TILE, tm)
    y = _experts(xs, tile_expert, used, *experts)
    out = _combine(y, pos, route, x1.reshape(t, d), g2, lng, lnb, n // tm, tm)
    return out.reshape(bsz, n, d)


def _gelu_tanh(x):
    return 0.5 * x * (1.0 + jnp.tanh(np.sqrt(2.0 / np.pi).astype(np.float32) * (x + 0.044715 * (x * x * x))))


def _cd_in_kernel(x_ref, sc_ref, sh_ref, w_ref, sg_ref, sb_ref, wsp_ref, bsp_ref, yc_ref, glu_ref):
    tm = x_ref.shape[1]
    h = (x_ref[0] * (1.0 + sc_ref[0]) + sh_ref[0]).astype(BF16)
    p = jnp.dot(h, w_ref[...], preferred_element_type=F32)
    for g in range(N_SGU_GROUPS):
        u = _gelu_tanh(p[:, g * SGU_GROUP_DIM:(g + 1) * SGU_GROUP_DIM])
        v = _gelu_tanh(p[:, C_W + g * SGU_GROUP_DIM:C_W + (g + 1) * SGU_GROUP_DIM])
        vg = _layer_norm(v, sg_ref[g:g + 1, :], sb_ref[g:g + 1, :]).astype(BF16)
        for c in range(tm // CHUNK):
            rows = slice(c * CHUNK, (c + 1) * CHUNK)
            sv = jnp.dot(wsp_ref[g], vg[rows], preferred_element_type=F32) + bsp_ref[g]
            yc_ref[0, rows, g * SGU_GROUP_DIM:(g + 1) * SGU_GROUP_DIM] = (u[rows] * sv).astype(BF16)
    a = p[:, 2 * C_W:2 * C_W + D_W]
    gate = p[:, 2 * C_W + D_W:]
    glu_ref[0] = a * jax.nn.sigmoid(gate)


def _cd_in_proj(x, sc, sh, w_bf, sg, sb, wsp_bf, bsp, tm):
    bsz, n, d = x.shape
    tm = min(tm, n)
    vec = pl.BlockSpec((1, 1, d), lambda b, i: (b, 0, 0))
    return pl.pallas_call(
        _cd_in_kernel,
        grid=(bsz, n // tm),
        in_specs=[pl.BlockSpec((1, tm, d), lambda b, i: (b, i, 0)), vec, vec,
                  pl.BlockSpec((d, CD_IN), lambda b, i: (0, 0)),
                  pl.BlockSpec((N_SGU_GROUPS, SGU_GROUP_DIM), lambda b, i: (0, 0)),
                  pl.BlockSpec((N_SGU_GROUPS, SGU_GROUP_DIM), lambda b, i: (0, 0)),
                  pl.BlockSpec((N_SGU_GROUPS, CHUNK, CHUNK), lambda b, i: (0, 0, 0)),
                  pl.BlockSpec((N_SGU_GROUPS, CHUNK, SGU_GROUP_DIM), lambda b, i: (0, 0, 0))],
        out_specs=[pl.BlockSpec((1, tm, C_W), lambda b, i: (b, i, 0)),
                   pl.BlockSpec((1, tm, D_W), lambda b, i: (b, i, 0))],
        out_shape=[jax.ShapeDtypeStruct((bsz, n, C_W), BF16),
                   jax.ShapeDtypeStruct((bsz, n, D_W), F32)],
        compiler_params=_params("parallel", "parallel"),
        name="cd_in_proj",
    )(x, sc, sh, w_bf, sg, sb, wsp_bf, bsp)


def _conv_kernel(glu_ref, w_ref, cb_ref, g_ref, b_ref, o_ref, pad_ref, *, rows):
    n = glu_ref.shape[1]
    halo = jnp.zeros((CONV_HALO, CONV_GROUP_DIM), F32)
    pad_ref[0:CONV_HALO, :] = halo
    pad_ref[CONV_HALO + n:CONV_HALO + n + CONV_HALO, :] = halo
    pad_ref[CONV_HALO:CONV_HALO + n, :] = glu_ref[0]
    first = CONV_HALO - CONV_WIDTH // 2
    for c in range(n // rows):
        acc = jnp.zeros((rows, CONV_GROUP_DIM), F32) + cb_ref[...]
        for k in range(CONV_WIDTH):
            start = c * rows + first + k
            acc = acc + w_ref[k:k + 1, :] * pad_ref[start:start + rows, :]
        dn = _layer_norm(acc, g_ref[...], b_ref[...])
        o_ref[0, c * rows:(c + 1) * rows, :] = _silu(dn).astype(BF16)


def _conv_module(glu, conv_w, conv_b, cn_g, cn_b, rows):
    bsz, n, _ = glu.shape
    rows = min(rows, n)
    vec = pl.BlockSpec((1, CONV_GROUP_DIM), lambda b, g: (0, g))
    return pl.pallas_call(
        functools.partial(_conv_kernel, rows=rows),
        grid=(bsz, N_CONV_GROUPS),
        in_specs=[pl.BlockSpec((1, n, CONV_GROUP_DIM), lambda b, g: (b, 0, g)),
                  pl.BlockSpec((CONV_WIDTH, CONV_GROUP_DIM), lambda b, g: (0, g)),
                  vec, vec, vec],
        out_specs=pl.BlockSpec((1, n, CONV_GROUP_DIM), lambda b, g: (b, 0, g)),
        out_shape=jax.ShapeDtypeStruct((bsz, n, D_W), BF16),
        scratch_shapes=[pltpu.VMEM((n + 2 * CONV_HALO, CONV_GROUP_DIM), F32)],
        compiler_params=_params("parallel", "parallel"),
        name="conv_module",
    )(glu, conv_w, conv_b, cn_g, cn_b)


def _head_pair_columns():
    inner = np.concatenate([np.arange(0, HEAD_DIM, 2), np.arange(1, HEAD_DIM, 2)])
    cols = []
    for t in range(N_KV_TILES):
        for r in range(REP):
            for g in (2 * t, 2 * t + 1):
                cols.append((g * REP + r) * HEAD_DIM + inner)
    for g in range(N_KV_HEADS):
        cols.append(Q_W + g * HEAD_DIM + inner)
    cols.append(np.arange(Q_W + KV_W, AB_IN))
    return np.concatenate(cols)


def _attn_out_rows():
    rows = []
    for t in range(N_KV_TILES):
        for r in range(REP):
            for g in (2 * t, 2 * t + 1):
                rows.append((g * REP + r) * HEAD_DIM + np.arange(HEAD_DIM))
    return np.concatenate(rows)


def _rope_tables(n, rotate):
    if not rotate:
        return jnp.ones((n, LANES), F32), jnp.zeros((n, LANES), F32)
    rows = n // GRID_W
    row = jnp.repeat(jnp.arange(rows, dtype=F32), GRID_W)
    col = jnp.tile(jnp.arange(GRID_W, dtype=F32), rows)
    inv_freq = 1.0 / (ROPE_THETA ** (jnp.arange(0, AXIS_DIM, 2, dtype=F32) / AXIS_DIM))
    ang = jnp.concatenate([row[:, None] * inv_freq, col[:, None] * inv_freq], axis=-1)
    cos, sin = jnp.cos(ang), jnp.sin(ang)
    return jnp.tile(cos, (1, 4)), jnp.tile(jnp.concatenate([-sin, sin], axis=-1), (1, 2))


def _pair_gain(g):
    perm = np.concatenate([np.arange(0, HEAD_DIM, 2), np.arange(1, HEAD_DIM, 2)])
    return jnp.tile(g[perm], 2).reshape(1, LANES)


def _block_diag(w):
    g, a, b = w.shape
    out = jnp.zeros((g * a, g * b), w.dtype)
    for i in range(g):
        out = out.at[i * a:(i + 1) * a, i * b:(i + 1) * b].set(w[i])
    return out


def _router_weights(w_group, b_group, w_router, b_router):
    d = w_group.shape[0]
    wr = jnp.concatenate([w_group, jnp.transpose(w_router, (1, 0, 2)).reshape(d, N_EXPERTS)], axis=1)
    br = jnp.concatenate([b_group, b_router.reshape(N_EXPERTS)])
    pad = ROUTER_LANES - wr.shape[1]
    return jnp.pad(wr, ((0, 0), (0, pad))), jnp.pad(br, (0, pad)).reshape(1, ROUTER_LANES)


def _vec(mod_l, k, rows):
    first, count, bsz = rows
    v = mod_l[first:first + count, k * D_MODEL:(k + 1) * D_MODEL][:, None, :]
    return jnp.broadcast_to(v, (bsz, 1, D_MODEL))


def _channel_mix(y_a, y_b, wa, wb, x, mod_l, rows, ln_g, ln_b, router, experts, tm):
    lng = lambda i: ln_g[i].reshape(1, D_MODEL)
    lnb = lambda i: ln_b[i].reshape(1, D_MODEL)
    x1, h2, route = _out_ln_route(y_a, y_b, wa, wb, x, _vec(mod_l, 2, rows), _vec(mod_l, 4, rows),
                                  _vec(mod_l, 3, rows), lng(0), lnb(0), *router, tm=tm)
    return _moe(h2, route, experts, x1, _vec(mod_l, 5, rows), lng(1), lnb(1))


def kernel(x, c, ctx, c_ctx, w_mod, b_mod, ln_g, ln_b, w_in_ab, w_out_ab, q_gain, k_gain, w_fourier, b_fourier, w_in_cd, w_out_cd, sgu_g, sgu_b, w_spatial, b_spatial, conv_w, conv_b, conv_norm_g, conv_norm_b, w_group, b_group, w_router, b_router, w_exp_gate, w_exp_up, w_exp_down):
    bsz, n, _ = x.shape
    lc = ctx.shape[1]
    depth = w_mod.shape[0]
    pad_rows = (-(bsz + 1)) % 8
    cond = jnp.concatenate([c, c_ctx[None, :], jnp.zeros((pad_rows, D_MODEL), F32)], axis=0)
    mod = _modulation(cond, w_mod, b_mod)
    lat_rows = (0, bsz, bsz)
    ctx_rows = (bsz, 1, bsz)

    in_cols = _head_pair_columns()
    out_rows = _attn_out_rows()
    rope_lat = _rope_tables(n, True)
    rope_ctx = _rope_tables(lc, False)
    dft = {m: _dft_tables(m) for m in {n, lc}}

    x_lat, x_ctx = x, ctx
    for l in range(depth):
        ctx_out = any(j % 2 == 0 for j in range(l + 1, depth))
        i = l // 2
        mod_l = mod[l]
        router = _router_weights(w_group[l], b_group[l], w_router[l], b_router[l])
        experts = (w_exp_gate[l].astype(BF16), w_exp_up[l].astype(BF16), w_exp_down[l].astype(BF16))
        if l % 2 == 0:
            w_in = w_in_ab[i][:, in_cols].astype(BF16)
            w_out = w_out_ab[i].astype(BF16)
            wa, wb = w_out[out_rows], w_out[Q_W:]
            qg, kg = _pair_gain(q_gain[i]), _pair_gain(k_gain[i])
            wf = _block_diag(w_fourier[i]).astype(BF16)
            bf = b_fourier[i].reshape(1, F_W)
            q, k, v, f = _ab_in_proj(x_lat, _vec(mod_l, 1, lat_rows), _vec(mod_l, 0, lat_rows),
                                     w_in, *rope_lat, qg, kg, tm=512)
            qc, kc, vc, fc = _ab_in_proj(x_ctx, _vec(mod_l, 1, ctx_rows), _vec(mod_l, 0, ctx_rows),
                                         w_in, *rope_ctx, qg, kg, tm=256)
            a_lat = _attention(q, [kc, k], [vc, v], tq=256)
            cs, wc = dft[n]
            fm_lat = _fourier_mix(f, jnp.asarray(wc).astype(BF16), jnp.asarray(cs).astype(BF16), wf, bf, tr=512)
            ya_lat, yb_lat = a_lat, fm_lat
            if ctx_out:
                ya_ctx = _attention(qc, [kc], [vc], tq=256)
                cs, wc = dft[lc]
                yb_ctx = _fourier_mix(fc, jnp.asarray(wc).astype(BF16), jnp.asarray(cs).astype(BF16), wf, bf, tr=256)
        else:
            w_in = w_in_cd[i].astype(BF16)
            w_out = w_out_cd[i].astype(BF16)
            wa, wb = w_out[:C_W], w_out[C_W:]
            wsp = w_spatial[i].astype(BF16)
            bsp = jnp.broadcast_to(b_spatial[i][:, :, None], (N_SGU_GROUPS, CHUNK, SGU_GROUP_DIM))
            cd = (conv_w[i], conv_b[i].reshape(1, D_W), conv_norm_g[i].reshape(1, D_W), conv_norm_b[i].reshape(1, D_W))
            ya_lat, glu = _cd_in_proj(x_lat, _vec(mod_l, 1, lat_rows), _vec(mod_l, 0, lat_rows),
                                      w_in, sgu_g[i], sgu_b[i], wsp, bsp, tm=256)
            yb_lat = _conv_module(glu, *cd, rows=256)
            if ctx_out:
                ya_ctx, glu_c = _cd_in_proj(x_ctx, _vec(mod_l, 1, ctx_rows), _vec(mod_l, 0, ctx_rows),
                                            w_in, sgu_g[i], sgu_b[i], wsp, bsp, tm=256)
                yb_ctx = _conv_module(glu_c, *cd, rows=256)
        x_lat = _channel_mix(ya_lat, yb_lat, wa, wb, x_lat, mod_l, lat_rows, ln_g[l], ln_b[l], router, experts, tm=256)
        if ctx_out:
            x_ctx = _channel_mix(ya_ctx, yb_ctx, wa, wb, x_ctx, mod_l, ctx_rows, ln_g[l], ln_b[l], router, experts, tm=256)
        else:
            x_ctx = None
    return x_lat
```

```python
import functools

import numpy as np
import jax
import jax.numpy as jnp
from jax import lax
from jax.experimental import pallas as pl
from jax.experimental.pallas import tpu as pltpu

F32 = jnp.float32
BF16 = jnp.bfloat16
U32 = jnp.uint32
HIGHEST = lax.Precision.HIGHEST

LANES = 128
D_MODEL = 1024
DEPTH = 4
GRID_W = 64
HEAD_DIM = 64
N_Q_HEADS = 12
N_KV_HEADS = 4
REP = N_Q_HEADS // N_KV_HEADS
ROPE_THETA = 10000.0
AXIS_DIM = HEAD_DIM // 2
N_FOURIER_GROUPS = 4
FOURIER_GROUP_DIM = 64
CHUNK = 128
N_SGU_GROUPS = 4
SGU_GROUP_DIM = 128
CONV_WIDTH = 31
CONV_HALO = 16
N_CONV_GROUPS = 4
CONV_GROUP_DIM = 128
N_EXPERT_GROUPS = 4
EXPERTS_PER_GROUP = 8
N_EXPERTS = N_EXPERT_GROUPS * EXPERTS_PER_GROUP
D_EXPERT = 256
Q_W = N_Q_HEADS * HEAD_DIM
KV_W = N_KV_HEADS * HEAD_DIM
F_W = N_FOURIER_GROUPS * FOURIER_GROUP_DIM
AB_IN = Q_W + 2 * KV_W + F_W
C_W = N_SGU_GROUPS * SGU_GROUP_DIM
D_W = N_CONV_GROUPS * CONV_GROUP_DIM
CD_IN = 2 * C_W + 2 * D_W
ALPHA = (2 * DEPTH) ** 0.25
EPS = 1e-6
N_Q_TILES = Q_W // LANES
N_KV_TILES = KV_W // LANES
LOG2E = 1.4426950408889634
Q_SCALE = LOG2E * HEAD_DIM ** -0.5
SCORE_BOUND_MAX = 60.0
ROUTER_LANES = LANES
TOP_K = 2
ROUTE_ID = 0
ROUTE_W = ROUTE_ID + TOP_K
EXPERT_TILE = 256
MOE_TOKEN_TILE = 256
ISSUE_UNROLL = 8
VMEM_LIMIT = 56 * 1024 * 1024


def _params(*sem):
    return pltpu.CompilerParams(dimension_semantics=sem, vmem_limit_bytes=VMEM_LIMIT)


def _silu(x):
    return x * jax.nn.sigmoid(x)


def _dot_split(x, w_hi, w_lo):
    x_hi = x.astype(BF16)
    x_lo = (x - x_hi.astype(F32)).astype(BF16)
    out = jnp.dot(x_hi, w_hi, preferred_element_type=F32)
    out = out + jnp.dot(x_lo, w_hi, preferred_element_type=F32)
    return out + jnp.dot(x_hi, w_lo, preferred_element_type=F32)


def _pack_bf16_pairs(x):
    half = x.shape[-1] // 2
    bits = lambda v: lax.bitcast_convert_type(v.astype(BF16).astype(F32), U32)
    return (bits(x[:, half:]) & U32(0xFFFF0000)) | lax.shift_right_logical(bits(x[:, :half]), U32(16))


def _unpack_bf16_pairs(w):
    lo = lax.bitcast_convert_type(lax.shift_left(w, U32(16)), F32)
    hi = lax.bitcast_convert_type(w & U32(0xFFFF0000), F32)
    return jnp.concatenate([lo, hi], axis=-1)


def _layer_norm(z, g, b):
    mu = jnp.mean(z, axis=-1, keepdims=True)
    zc = z - mu
    var = jnp.mean(zc * zc, axis=-1, keepdims=True)
    return zc * lax.rsqrt(var + EPS) * g + b


def _mod_kernel(s_ref, w_ref, b_ref, o_ref):
    s = _silu(s_ref[...])
    o_ref[0] = jnp.dot(s, w_ref[0], preferred_element_type=F32, precision=HIGHEST) + b_ref[0]


def _modulation(cond, w_mod, b_mod):
    n_layers, d, six_d = w_mod.shape
    rows = cond.shape[0]
    tn = 1536
    return pl.pallas_call(
        _mod_kernel,
        grid=(n_layers, six_d // tn),
        in_specs=[pl.BlockSpec((rows, d), lambda l, j: (0, 0)),
                  pl.BlockSpec((1, d, tn), lambda l, j: (l, 0, j)),
                  pl.BlockSpec((1, 1, tn), lambda l, j: (l, 0, j))],
        out_specs=pl.BlockSpec((1, rows, tn), lambda l, j: (l, 0, j)),
        out_shape=jax.ShapeDtypeStruct((n_layers, rows, six_d), F32),
        compiler_params=_params("parallel", "parallel"),
        name="modulation",
    )(cond, w_mod, b_mod.reshape(n_layers, 1, six_d))


def _ab_in_kernel(x_ref, sc_ref, sh_ref, w_ref, cos_ref, sin_ref, qg_ref, kg_ref,
                  q_ref, k_ref, v_ref, f_ref):
    tm = x_ref.shape[1]
    h = (x_ref[0] * (1.0 + sc_ref[0]) + sh_ref[0]).astype(BF16)
    p = jnp.dot(h, w_ref[...], preferred_element_type=F32)
    lane = lax.broadcasted_iota(jnp.int32, (tm, LANES), 1)
    low_head = lane < HEAD_DIM
    first_half = (lane % HEAD_DIM) < AXIS_DIM
    cos = cos_ref[...]
    sin = sin_ref[...]

    def norm_rope(t, gain):
        sq = t * t
        ss_lo = jnp.sum(jnp.where(low_head, sq, 0.0), axis=-1, keepdims=True)
        ss_hi = jnp.sum(jnp.where(low_head, 0.0, sq), axis=-1, keepdims=True)
        ms = jnp.where(low_head, ss_lo, ss_hi) * (1.0 / HEAD_DIM)
        tn = t * lax.rsqrt(ms + EPS) * gain
        partner = jnp.where(first_half,
                            pltpu.roll(tn, LANES - AXIS_DIM, 1),
                            pltpu.roll(tn, AXIS_DIM, 1))
        return tn * cos + partner * sin

    for j in range(N_Q_TILES):
        t = norm_rope(p[:, j * LANES:(j + 1) * LANES], qg_ref[...])
        q_ref[0, j] = (t * Q_SCALE).astype(BF16)
    for j in range(N_KV_TILES):
        c0 = Q_W + j * LANES
        k_ref[0, j] = norm_rope(p[:, c0:c0 + LANES], kg_ref[...]).astype(BF16)
        c1 = Q_W + KV_W + j * LANES
        v_ref[0, j] = p[:, c1:c1 + LANES].astype(BF16)
    f_ref[0] = p[:, Q_W + 2 * KV_W:].astype(BF16)


def _ab_in_proj(x, sc, sh, w_bf, cos, sin, qg, kg, tm):
    bsz, n, d = x.shape
    tm = min(tm, n)
    vec = pl.BlockSpec((1, 1, d), lambda b, i: (b, 0, 0))
    tab = pl.BlockSpec((tm, LANES), lambda b, i: (i, 0))
    gain = pl.BlockSpec((1, LANES), lambda b, i: (0, 0))
    return pl.pallas_call(
        _ab_in_kernel,
        grid=(bsz, n // tm),
        in_specs=[pl.BlockSpec((1, tm, d), lambda b, i: (b, i, 0)), vec, vec,
                  pl.BlockSpec((d, AB_IN), lambda b, i: (0, 0)), tab, tab, gain, gain],
        out_specs=[pl.BlockSpec((1, N_Q_TILES, tm, LANES), lambda b, i: (b, 0, i, 0)),
                   pl.BlockSpec((1, N_KV_TILES, tm, LANES), lambda b, i: (b, 0, i, 0)),
                   pl.BlockSpec((1, N_KV_TILES, tm, LANES), lambda b, i: (b, 0, i, 0)),
                   pl.BlockSpec((1, tm, F_W), lambda b, i: (b, i, 0))],
        out_shape=[jax.ShapeDtypeStruct((bsz, N_Q_TILES, n, LANES), BF16),
                   jax.ShapeDtypeStruct((bsz, N_KV_TILES, n, LANES), BF16),
                   jax.ShapeDtypeStruct((bsz, N_KV_TILES, n, LANES), BF16),
                   jax.ShapeDtypeStruct((bsz, n, F_W), BF16)],
        compiler_params=_params("parallel", "parallel"),
        name="ab_in_proj",
    )(x, sc, sh, w_bf, cos, sin, qg, kg)


def _attn_kernel(bound_ref, *refs, n_seg, row_max):
    q_ref = refs[0]
    k_refs = refs[1:1 + n_seg]
    v_refs = refs[1 + n_seg:1 + 2 * n_seg]
    o_ref = refs[1 + 2 * n_seg]
    tq = q_ref.shape[2]
    lane = lax.broadcasted_iota(jnp.int32, (tq, LANES), 1)
    low_head = lane < HEAD_DIM
    zero = jnp.zeros((), BF16)
    dn = (((1,), (1,)), ((), ()))
    for j in range(REP):
        q = q_ref[0, j]
        q2 = jnp.concatenate([jnp.where(low_head, q, zero), jnp.where(low_head, zero, q)], axis=0)
        s = [lax.dot_general(q2, k_ref[0, 0], dn, preferred_element_type=F32) for k_ref in k_refs]
        if row_max:
            m = s[0].max(axis=-1, keepdims=True)
            for si in s[1:]:
                m = jnp.maximum(m, si.max(axis=-1, keepdims=True))
        else:
            m = bound_ref[0]
        den = jnp.zeros((2 * tq, LANES), F32)
        o2 = jnp.zeros((2 * tq, LANES), F32)
        for si, v_ref in zip(s, v_refs):
            p = jnp.exp2(si - m)
            for t in range(si.shape[1] // LANES):
                den = den + p[:, t * LANES:(t + 1) * LANES]
            o2 = o2 + jnp.dot(p.astype(BF16), v_ref[0, 0], preferred_element_type=F32)
        o2 = o2 / jnp.sum(den, axis=-1, keepdims=True)
        o_ref[0, :, j * LANES:(j + 1) * LANES] = jnp.where(low_head, o2[:tq], o2[tq:]).astype(BF16)


def _attention_call(bound, q, ks, vs, tq, row_max):
    bsz, _, n, _ = q.shape
    tq = min(tq, n)
    n_seg = len(ks)
    kv_specs = [pl.BlockSpec((1, 1, a.shape[2], LANES), lambda b, i, t, bound: (b, t, 0, 0)) for a in ks + vs]
    return pl.pallas_call(
        functools.partial(_attn_kernel, n_seg=n_seg, row_max=row_max),
        grid_spec=pltpu.PrefetchScalarGridSpec(
            num_scalar_prefetch=1,
            grid=(bsz, n // tq, N_KV_TILES),
            in_specs=[pl.BlockSpec((1, REP, tq, LANES), lambda b, i, t, bound: (b, t, i, 0))] + kv_specs,
            out_specs=pl.BlockSpec((1, tq, REP * LANES), lambda b, i, t, bound: (b, i, t))),
        out_shape=jax.ShapeDtypeStruct((bsz, n, Q_W), BF16),
        compiler_params=_params("parallel", "parallel", "arbitrary"),
        name="attention_rowmax" if row_max else "attention",
    )(bound, q, *ks, *vs)


def _attention(q, ks, vs, bound, tq):
    n_seg = len(ks)

    def run(row_max):
        return lambda bound, q, *kv: _attention_call(bound, q, list(kv[:n_seg]), list(kv[n_seg:]), tq, row_max)

    return lax.cond(bound[0] <= SCORE_BOUND_MAX, run(False), run(True), bound, q, *ks, *vs)


def _fourier_kernel(f_ref, wc_ref, cs_ref, wf_ref, bf_ref, o_ref, y_ref):
    n = f_ref.shape[1]

    @pl.when(pl.program_id(1) == 0)
    def _():
        y = jnp.dot(f_ref[0], wc_ref[...], preferred_element_type=F32)
        y_ref[0:n, :] = y[:, :F_W].astype(BF16)
        y_ref[n:2 * n, :] = y[:, F_W:].astype(BF16)

    mixed = jnp.dot(cs_ref[...], y_ref[...], preferred_element_type=F32)
    out = jnp.dot(mixed.astype(BF16), wf_ref[...], preferred_element_type=F32) + bf_ref[...]
    o_ref[0] = out.astype(BF16)


def _fourier_mix(f, wc, cs, wf, bf, tr):
    bsz, n, _ = f.shape
    tr = min(tr, n)
    return pl.pallas_call(
        _fourier_kernel,
        grid=(bsz, n // tr),
        in_specs=[pl.BlockSpec((1, n, F_W), lambda b, i: (b, 0, 0)),
                  pl.BlockSpec((F_W, 2 * F_W), lambda b, i: (0, 0)),
                  pl.BlockSpec((tr, 2 * n), lambda b, i: (i, 0)),
                  pl.BlockSpec((F_W, F_W), lambda b, i: (0, 0)),
                  pl.BlockSpec((1, F_W), lambda b, i: (0, 0))],
        out_specs=pl.BlockSpec((1, tr, F_W), lambda b, i: (b, i, 0)),
        out_shape=jax.ShapeDtypeStruct((bsz, n, F_W), BF16),
        scratch_shapes=[pltpu.VMEM((2 * n, F_W), BF16)],
        compiler_params=_params("parallel", "arbitrary"),
        name="fourier_mix",
    )(f, wc, cs, wf, bf)


def _dft_tables(n):
    k = np.arange(n, dtype=np.int64)
    ang = 2.0 * np.pi * ((k[:, None] * k[None, :]) % n).astype(np.float64) / n
    scale = 1.0 / np.sqrt(float(n) * FOURIER_GROUP_DIM)
    cs = np.concatenate([np.cos(ang), -np.sin(ang)], axis=1) * scale
    c = np.arange(FOURIER_GROUP_DIM, dtype=np.int64)
    angc = 2.0 * np.pi * ((c[:, None] * c[None, :]) % FOURIER_GROUP_DIM).astype(np.float64) / FOURIER_GROUP_DIM
    eye = np.eye(N_FOURIER_GROUPS)
    wc = np.concatenate([np.kron(eye, np.cos(angc)), np.kron(eye, np.sin(angc))], axis=1)
    return cs.astype(np.float32), wc.astype(np.float32)


def _route(lg):
    lane = lax.broadcasted_iota(jnp.int32, lg.shape, 1).astype(F32)
    neg = jnp.float32(-1e30)
    first = lambda hit: jnp.min(jnp.where(hit, lane, float(LANES)), axis=-1, keepdims=True)
    gl = jnp.where(lane < N_EXPERT_GROUPS, lg, neg)
    gmax = gl.max(axis=-1, keepdims=True)
    g_idx = first(gl == gmax)
    g_w = 1.0 / jnp.sum(jnp.exp(gl - gmax), axis=-1, keepdims=True)
    lo = N_EXPERT_GROUPS + EXPERTS_PER_GROUP * g_idx
    el = jnp.where((lane >= lo) & (lane < lo + EXPERTS_PER_GROUP), lg, neg)
    v1 = el.max(axis=-1, keepdims=True)
    i1 = first(el == v1)
    el2 = jnp.where(lane == i1, neg, el)
    v2 = el2.max(axis=-1, keepdims=True)
    i2 = first(el2 == v2)
    e2 = jnp.exp(v2 - v1)
    w1 = g_w / (1.0 + e2)
    w2 = g_w * e2 / (1.0 + e2)
    out = jnp.where(lane == ROUTE_ID, i1 - N_EXPERT_GROUPS, 0.0)
    out = out + jnp.where(lane == ROUTE_ID + 1, i2 - N_EXPERT_GROUPS, 0.0)
    return out + jnp.where(lane == ROUTE_W, w1, 0.0) + jnp.where(lane == ROUTE_W + 1, w2, 0.0)


def _out_ln_route_kernel(a_ref, b_ref, wa_ref, wb_ref, x_ref, g1_ref, sc2_ref, sh2_ref,
                         lng_ref, lnb_ref, wrh_ref, wrl_ref, br_ref, x1_ref, h2_ref, route_ref):
    y = jnp.dot(a_ref[0], wa_ref[...], preferred_element_type=F32)
    y = y + jnp.dot(b_ref[0], wb_ref[...], preferred_element_type=F32)
    x1 = _layer_norm(ALPHA * x_ref[0] + g1_ref[0] * y, lng_ref[...], lnb_ref[...])
    x1_ref[0] = x1
    h2 = x1 * (1.0 + sc2_ref[0]) + sh2_ref[0]
    h2_ref[0] = _pack_bf16_pairs(h2)
    lg = _dot_split(h2, wrh_ref[...], wrl_ref[...]) + br_ref[...]
    route_ref[0] = _route(lg)


def _out_ln_route(a, b, wa, wb, x, g1, sc2, sh2, lng, lnb, wrh, wrl, br, tm):
    bsz, n, d = x.shape
    tm = min(tm, n)
    ka, kb = a.shape[-1], b.shape[-1]
    vec = pl.BlockSpec((1, 1, d), lambda bb, i: (bb, 0, 0))
    row = pl.BlockSpec((1, d), lambda bb, i: (0, 0))
    tile = lambda w: pl.BlockSpec((1, tm, w), lambda bb, i: (bb, i, 0))
    full = lambda r, c: pl.BlockSpec((r, c), lambda bb, i: (0, 0))
    return pl.pallas_call(
        _out_ln_route_kernel,
        grid=(bsz, n // tm),
        in_specs=[tile(ka), tile(kb), full(ka, d), full(kb, d), tile(d), vec, vec, vec, row, row,
                  full(d, ROUTER_LANES), full(d, ROUTER_LANES), full(1, ROUTER_LANES)],
        out_specs=[tile(d), tile(d // 2), tile(ROUTER_LANES)],
        out_shape=[jax.ShapeDtypeStruct((bsz, n, d), F32),
                   jax.ShapeDtypeStruct((bsz, n, d // 2), U32),
                   jax.ShapeDtypeStruct((bsz, n, ROUTER_LANES), F32)],
        compiler_params=_params("parallel", "parallel"),
        name="out_ln_route",
    )(a, b, wa, wb, x, g1, sc2, sh2, lng, lnb, wrh, wrl, br)


def _moe_plan(route, n_tiles):
    ids = route[:, ROUTE_ID:ROUTE_ID + TOP_K].astype(jnp.int32)
    flat = ids.T.reshape(-1)
    onehot = (flat[:, None] == jnp.arange(N_EXPERTS, dtype=jnp.int32)[None, :]).astype(jnp.int32)
    csum = jnp.cumsum(onehot, axis=0)
    counts = csum[-1]
    padded = ((counts + EXPERT_TILE - 1) // EXPERT_TILE) * EXPERT_TILE
    ends = jnp.cumsum(padded)
    starts = ends - padded
    pos = jnp.sum((csum - onehot + starts[None, :]) * onehot, axis=1)
    tile_start = jnp.arange(n_tiles, dtype=jnp.int32) * EXPERT_TILE
    tile_expert = jnp.minimum(jnp.sum((tile_start[:, None] >= ends[None, :]).astype(jnp.int32), axis=1),
                              N_EXPERTS - 1)
    used = (ends[-1] // EXPERT_TILE).reshape(1)
    return pos.reshape(TOP_K, -1), ends, tile_expert, used


def _row_copy(src, src_row, dst, dst_row, sem):
    return pltpu.make_async_copy(src.at[pl.ds(src_row, 1), :], dst.at[pl.ds(dst_row, 1), :], sem)


def _dispatch_kernel(ends_ref, pos_ref, h_ref, x_hbm, zero_ref, sem, zsem):
    tm = h_ref.shape[0]

    @pl.when(pl.program_id(0) == 0)
    def _():
        zero_ref[...] = jnp.zeros_like(zero_ref)
        for e in range(N_EXPERTS):
            end = ends_ref[e]
            begin = ends_ref[e - 1] if e else 0

            @pl.when(end > begin)
            def _():
                start = pl.multiple_of(end - EXPERT_TILE, EXPERT_TILE)
                cp = pltpu.make_async_copy(zero_ref, x_hbm.at[pl.ds(start, EXPERT_TILE), :], zsem)
                cp.start()
                cp.wait()

    def issue(i, carry):
        for u in range(ISSUE_UNROLL):
            r = i * ISSUE_UNROLL + u
            for k in range(TOP_K):
                _row_copy(h_ref, r, x_hbm, pos_ref[0, 0, k * tm + r], sem).start()
        return carry

    def drain(i, carry):
        for _ in range(ISSUE_UNROLL * TOP_K):
            _row_copy(h_ref, 0, x_hbm, 0, sem).wait()
        return carry

    lax.fori_loop(0, tm // ISSUE_UNROLL, issue, 0)
    lax.fori_loop(0, tm // ISSUE_UNROLL, drain, 0)


def _dispatch(h2, pos, ends, n_rows, tm):
    t, d = h2.shape
    pos_blocks = pos.reshape(TOP_K, t // tm, tm).transpose(1, 0, 2).reshape(t // tm, 1, TOP_K * tm)
    return pl.pallas_call(
        _dispatch_kernel,
        grid_spec=pltpu.PrefetchScalarGridSpec(
            num_scalar_prefetch=1,
            grid=(t // tm,),
            in_specs=[pl.BlockSpec((1, 1, TOP_K * tm), lambda i, ends: (i, 0, 0), memory_space=pltpu.SMEM),
                      pl.BlockSpec((tm, d), lambda i, ends: (i, 0))],
            out_specs=pl.BlockSpec(memory_space=pl.ANY),
            scratch_shapes=[pltpu.VMEM((EXPERT_TILE, d), h2.dtype),
                            pltpu.SemaphoreType.DMA, pltpu.SemaphoreType.DMA]),
        out_shape=jax.ShapeDtypeStruct((n_rows, d), h2.dtype),
        compiler_params=_params("arbitrary"),
        name="moe_dispatch",
    )(ends, pos_blocks, h2)


def _expert_kernel(te_ref, used_ref, x_ref, wg_ref, wu_ref, wd_ref, y_ref, wg_bf, wu_bf, wd_bf):
    j = pl.program_id(0)
    live = j < used_ref[0]
    new_expert = (j == 0) | (te_ref[j] != te_ref[jnp.maximum(j - 1, 0)])

    @pl.when(live & new_expert)
    def _():
        wg_bf[...] = wg_ref[0, 0].astype(BF16)
        wu_bf[...] = wu_ref[0, 0].astype(BF16)
        wd_bf[...] = wd_ref[0, 0].astype(BF16)

    @pl.when(live)
    def _():
        x = _unpack_bf16_pairs(x_ref[...]).astype(BF16)
        gate = jnp.dot(x, wg_bf[...], preferred_element_type=F32)
        up = jnp.dot(x, wu_bf[...], preferred_element_type=F32)
        hid = (_silu(gate) * up).astype(BF16)
        y_ref[...] = _pack_bf16_pairs(jnp.dot(hid, wd_bf[...], preferred_element_type=F32))

    @pl.when(jnp.logical_not(live))
    def _():
        y_ref[...] = jnp.zeros_like(y_ref)


def _experts(xs, tile_expert, used, wg, wu, wd, layer):
    n_rows, half = xs.shape
    d = 2 * half
    n_tiles = n_rows // EXPERT_TILE
    by_expert = lambda j, te, used: (layer, te[j], 0, 0)
    return pl.pallas_call(
        _expert_kernel,
        grid_spec=pltpu.PrefetchScalarGridSpec(
            num_scalar_prefetch=2,
            grid=(n_tiles,),
            in_specs=[pl.BlockSpec((EXPERT_TILE, half), lambda j, te, used: (jnp.minimum(j, used[0] - 1), 0)),
                      pl.BlockSpec((1, 1, d, D_EXPERT), by_expert),
                      pl.BlockSpec((1, 1, d, D_EXPERT), by_expert),
                      pl.BlockSpec((1, 1, D_EXPERT, d), by_expert)],
            out_specs=pl.BlockSpec((EXPERT_TILE, half), lambda j, te, used: (j, 0)),
            scratch_shapes=[pltpu.VMEM((d, D_EXPERT), BF16), pltpu.VMEM((d, D_EXPERT), BF16),
                            pltpu.VMEM((D_EXPERT, d), BF16)]),
        out_shape=jax.ShapeDtypeStruct((n_rows, half), U32),
        compiler_params=_params("arbitrary"),
        name="moe_experts",
    )(tile_expert, used, xs, wg, wu, wd)


def _combine_kernel(pos_ref, next_pos_ref, route_ref, x1_ref, g2_ref, lng_ref, lnb_ref, y_hbm, o_ref, ybuf, sems):
    tm = x1_ref.shape[0]
    i = pl.program_id(0)
    slot = i % 2

    def gather(idx_ref, s):
        def issue(it, carry):
            for u in range(ISSUE_UNROLL):
                r = it * ISSUE_UNROLL + u
                for k in range(TOP_K):
                    _row_copy(y_hbm, idx_ref[0, 0, k * tm + r], ybuf.at[s, k], r, sems.at[s]).start()
            return carry

        lax.fori_loop(0, tm // ISSUE_UNROLL, issue, 0)

    @pl.when(i == 0)
    def _():
        gather(pos_ref, slot)

    @pl.when(i + 1 < pl.num_programs(0))
    def _():
        gather(next_pos_ref, 1 - slot)

    def drain(it, carry):
        for _ in range(ISSUE_UNROLL * TOP_K):
            _row_copy(y_hbm, 0, ybuf.at[slot, 0], 0, sems.at[slot]).wait()
        return carry

    lax.fori_loop(0, tm // ISSUE_UNROLL, drain, 0)

    route = route_ref[...]
    lane = lax.broadcasted_iota(jnp.int32, route.shape, 1)
    y = jnp.zeros_like(o_ref)
    for k in range(TOP_K):
        w = jnp.sum(jnp.where(lane == ROUTE_W + k, route, 0.0), axis=-1, keepdims=True)
        y = y + w * _unpack_bf16_pairs(ybuf[slot, k])
    z = ALPHA * x1_ref[...] + g2_ref[0] * y
    o_ref[...] = _layer_norm(z, lng_ref[...], lnb_ref[...])


def _combine(y, pos, route, x1, g2, lng, lnb, tiles_per_batch, tm):
    t, d = x1.shape
    steps = t // tm
    pos_blocks = pos.reshape(TOP_K, steps, tm).transpose(1, 0, 2).reshape(steps, 1, TOP_K * tm)
    return pl.pallas_call(
        _combine_kernel,
        grid=(steps,),
        in_specs=[pl.BlockSpec((1, 1, TOP_K * tm), lambda i: (i, 0, 0), memory_space=pltpu.SMEM),
                  pl.BlockSpec((1, 1, TOP_K * tm), lambda i: (jnp.minimum(i + 1, steps - 1), 0, 0),
                               memory_space=pltpu.SMEM),
                  pl.BlockSpec((tm, ROUTER_LANES), lambda i: (i, 0)),
                  pl.BlockSpec((tm, d), lambda i: (i, 0)),
                  pl.BlockSpec((1, 1, d), lambda i: (i // tiles_per_batch, 0, 0)),
                  pl.BlockSpec((1, d), lambda i: (0, 0)),
                  pl.BlockSpec((1, d), lambda i: (0, 0)),
                  pl.BlockSpec(memory_space=pl.ANY)],
        out_specs=pl.BlockSpec((tm, d), lambda i: (i, 0)),
        out_shape=jax.ShapeDtypeStruct((t, d), F32),
        scratch_shapes=[pltpu.VMEM((2, TOP_K, tm, d // 2), U32), pltpu.SemaphoreType.DMA((2,))],
        compiler_params=_params("arbitrary"),
        name="moe_combine",
    )(pos_blocks, pos_blocks, route, x1, g2, lng, lnb, y)


def _moe(h2, route, experts, layer, x1, g2, lng, lnb):
    bsz, n, d = x1.shape
    t = bsz * n
    tm = min(MOE_TOKEN_TILE, n)
    n_tiles = TOP_K * t // EXPERT_TILE + N_EXPERTS
    route = route.reshape(t, ROUTER_LANES)
    pos, ends, tile_expert, used = _moe_plan(route, n_tiles)
    xs = _dispatch(h2.reshape(t, d // 2), pos, ends, n_tiles * EXPERT_TILE, tm)
    y = _experts(xs, tile_expert, used, *experts, layer)
    out = _combine(y, pos, route, x1.reshape(t, d), g2, lng, lnb, n // tm, tm)
    return out.reshape(bsz, n, d)


def _gelu_tanh(x):
    return 0.5 * x * (1.0 + jnp.tanh(np.sqrt(2.0 / np.pi).astype(np.float32) * (x + 0.044715 * (x * x * x))))


def _cd_in_kernel(x_ref, sc_ref, sh_ref, w_ref, sg_ref, sb_ref, wsp_ref, bsp_ref, yc_ref, glu_ref):
    tm = x_ref.shape[1]
    h = (x_ref[0] * (1.0 + sc_ref[0]) + sh_ref[0]).astype(BF16)
    p = jnp.dot(h, w_ref[...], preferred_element_type=F32)
    for g in range(N_SGU_GROUPS):
        u = _gelu_tanh(p[:, g * SGU_GROUP_DIM:(g + 1) * SGU_GROUP_DIM])
        v = _gelu_tanh(p[:, C_W + g * SGU_GROUP_DIM:C_W + (g + 1) * SGU_GROUP_DIM])
        vg = _layer_norm(v, sg_ref[g:g + 1, :], sb_ref[g:g + 1, :]).astype(BF16)
        for c in range(tm // CHUNK):
            rows = slice(c * CHUNK, (c + 1) * CHUNK)
            sv = jnp.dot(wsp_ref[g], vg[rows], preferred_element_type=F32) + bsp_ref[g]
            yc_ref[0, rows, g * SGU_GROUP_DIM:(g + 1) * SGU_GROUP_DIM] = (u[rows] * sv).astype(BF16)
    a = p[:, 2 * C_W:2 * C_W + D_W]
    gate = p[:, 2 * C_W + D_W:]
    glu_ref[0] = a * jax.nn.sigmoid(gate)


def _cd_in_proj(x, sc, sh, w_bf, sg, sb, wsp_bf, bsp, tm):
    bsz, n, d = x.shape
    tm = min(tm, n)
    vec = pl.BlockSpec((1, 1, d), lambda b, i: (b, 0, 0))
    return pl.pallas_call(
        _cd_in_kernel,
        grid=(bsz, n // tm),
        in_specs=[pl.BlockSpec((1, tm, d), lambda b, i: (b, i, 0)), vec, vec,
                  pl.BlockSpec((d, CD_IN), lambda b, i: (0, 0)),
                  pl.BlockSpec((N_SGU_GROUPS, SGU_GROUP_DIM), lambda b, i: (0, 0)),
                  pl.BlockSpec((N_SGU_GROUPS, SGU_GROUP_DIM), lambda b, i: (0, 0)),
                  pl.BlockSpec((N_SGU_GROUPS, CHUNK, CHUNK), lambda b, i: (0, 0, 0)),
                  pl.BlockSpec((N_SGU_GROUPS, CHUNK, SGU_GROUP_DIM), lambda b, i: (0, 0, 0))],
        out_specs=[pl.BlockSpec((1, tm, C_W), lambda b, i: (b, i, 0)),
                   pl.BlockSpec((1, tm, D_W), lambda b, i: (b, i, 0))],
        out_shape=[jax.ShapeDtypeStruct((bsz, n, C_W), BF16),
                   jax.ShapeDtypeStruct((bsz, n, D_W), F32)],
        compiler_params=_params("parallel", "parallel"),
        name="cd_in_proj",
    )(x, sc, sh, w_bf, sg, sb, wsp_bf, bsp)


def _conv_kernel(glu_ref, w_ref, cb_ref, g_ref, b_ref, o_ref, pad_ref, *, rows):
    n = glu_ref.shape[1]
    halo = jnp.zeros((CONV_HALO, CONV_GROUP_DIM), F32)
    pad_ref[0:CONV_HALO, :] = halo
    pad_ref[CONV_HALO + n:CONV_HALO + n + CONV_HALO, :] = halo
    pad_ref[CONV_HALO:CONV_HALO + n, :] = glu_ref[0]
    first = CONV_HALO - CONV_WIDTH // 2
    for c in range(n // rows):
        acc = jnp.zeros((rows, CONV_GROUP_DIM), F32) + cb_ref[...]
        for k in range(CONV_WIDTH):
            start = c * rows + first + k
            acc = acc + w_ref[k:k + 1, :] * pad_ref[start:start + rows, :]
        dn = _layer_norm(acc, g_ref[...], b_ref[...])
        o_ref[0, c * rows:(c + 1) * rows, :] = _silu(dn).astype(BF16)


def _conv_module(glu, conv_w, conv_b, cn_g, cn_b, rows):
    bsz, n, _ = glu.shape
    rows = min(rows, n)
    vec = pl.BlockSpec((1, CONV_GROUP_DIM), lambda b, g: (0, g))
    return pl.pallas_call(
        functools.partial(_conv_kernel, rows=rows),
        grid=(bsz, N_CONV_GROUPS),
        in_specs=[pl.BlockSpec((1, n, CONV_GROUP_DIM), lambda b, g: (b, 0, g)),
                  pl.BlockSpec((CONV_WIDTH, CONV_GROUP_DIM), lambda b, g: (0, g)),
                  vec, vec, vec],
        out_specs=pl.BlockSpec((1, n, CONV_GROUP_DIM), lambda b, g: (b, 0, g)),
        out_shape=jax.ShapeDtypeStruct((bsz, n, D_W), BF16),
        scratch_shapes=[pltpu.VMEM((n + 2 * CONV_HALO, CONV_GROUP_DIM), F32)],
        compiler_params=_params("parallel", "parallel"),
        name="conv_module",
    )(glu, conv_w, conv_b, cn_g, cn_b)


def _head_pair_columns():
    inner = np.concatenate([np.arange(0, HEAD_DIM, 2), np.arange(1, HEAD_DIM, 2)])
    cols = []
    for t in range(N_KV_TILES):
        for r in range(REP):
            for g in (2 * t, 2 * t + 1):
                cols.append((g * REP + r) * HEAD_DIM + inner)
    for g in range(N_KV_HEADS):
        cols.append(Q_W + g * HEAD_DIM + inner)
    cols.append(np.arange(Q_W + KV_W, AB_IN))
    return np.concatenate(cols)


def _attn_out_rows():
    rows = []
    for t in range(N_KV_TILES):
        for r in range(REP):
            for g in (2 * t, 2 * t + 1):
                rows.append((g * REP + r) * HEAD_DIM + np.arange(HEAD_DIM))
    return np.concatenate(rows)


def _rope_tables(n, rotate):
    if not rotate:
        return jnp.ones((n, LANES), F32), jnp.zeros((n, LANES), F32)
    rows = n // GRID_W
    row = jnp.repeat(jnp.arange(rows, dtype=F32), GRID_W)
    col = jnp.tile(jnp.arange(GRID_W, dtype=F32), rows)
    inv_freq = 1.0 / (ROPE_THETA ** (jnp.arange(0, AXIS_DIM, 2, dtype=F32) / AXIS_DIM))
    ang = jnp.concatenate([row[:, None] * inv_freq, col[:, None] * inv_freq], axis=-1)
    cos, sin = jnp.cos(ang), jnp.sin(ang)
    return jnp.tile(cos, (1, 4)), jnp.tile(jnp.concatenate([-sin, sin], axis=-1), (1, 2))


def _pair_gain(g):
    perm = np.concatenate([np.arange(0, HEAD_DIM, 2), np.arange(1, HEAD_DIM, 2)])
    return jnp.tile(g[perm], 2).reshape(1, LANES)


def _score_bound(qg, kg):
    bound = 1.02 * HEAD_DIM * Q_SCALE * jnp.max(jnp.abs(qg)) * jnp.max(jnp.abs(kg))
    return bound.reshape(1).astype(F32)


def _block_diag(w):
    g, a, b = w.shape
    out = jnp.zeros((g * a, g * b), w.dtype)
    for i in range(g):
        out = out.at[i * a:(i + 1) * a, i * b:(i + 1) * b].set(w[i])
    return out


def _router_weights(w_group, b_group, w_router, b_router):
    d = w_group.shape[0]
    wr = jnp.concatenate([w_group, jnp.transpose(w_router, (1, 0, 2)).reshape(d, N_EXPERTS)], axis=1)
    br = jnp.concatenate([b_group, b_router.reshape(N_EXPERTS)])
    pad = ROUTER_LANES - wr.shape[1]
    wr = jnp.pad(wr, ((0, 0), (0, pad)))
    wr_hi = wr.astype(BF16)
    wr_lo = (wr - wr_hi.astype(F32)).astype(BF16)
    return wr_hi, wr_lo, jnp.pad(br, (0, pad)).reshape(1, ROUTER_LANES)


def _vec(mod_l, k, rows):
    first, count, bsz = rows
    v = mod_l[first:first + count, k * D_MODEL:(k + 1) * D_MODEL][:, None, :]
    return jnp.broadcast_to(v, (bsz, 1, D_MODEL))


def _channel_mix(y_a, y_b, wa, wb, x, mod_l, rows, ln_g, ln_b, router, experts, layer, tm):
    lng = lambda i: ln_g[i].reshape(1, D_MODEL)
    lnb = lambda i: ln_b[i].reshape(1, D_MODEL)
    x1, h2, route = _out_ln_route(y_a, y_b, wa, wb, x, _vec(mod_l, 2, rows), _vec(mod_l, 4, rows),
                                  _vec(mod_l, 3, rows), lng(0), lnb(0), *router, tm=tm)
    return _moe(h2, route, experts, layer, x1, _vec(mod_l, 5, rows), lng(1), lnb(1))


def kernel(x, c, ctx, c_ctx, w_mod, b_mod, ln_g, ln_b, w_in_ab, w_out_ab, q_gain, k_gain, w_fourier, b_fourier, w_in_cd, w_out_cd, sgu_g, sgu_b, w_spatial, b_spatial, conv_w, conv_b, conv_norm_g, conv_norm_b, w_group, b_group, w_router, b_router, w_exp_gate, w_exp_up, w_exp_down):
    bsz, n, _ = x.shape
    lc = ctx.shape[1]
    depth = w_mod.shape[0]
    pad_rows = (-(bsz + 1)) % 8
    cond = jnp.concatenate([c, c_ctx[None, :], jnp.zeros((pad_rows, D_MODEL), F32)], axis=0)
    mod = _modulation(cond, w_mod, b_mod)
    lat_rows = (0, bsz, bsz)
    ctx_rows = (bsz, 1, bsz)

    in_cols = _head_pair_columns()
    out_rows = _attn_out_rows()
    rope_lat = _rope_tables(n, True)
    rope_ctx = _rope_tables(lc, False)
    dft = {m: _dft_tables(m) for m in {n, lc}}

    experts = (w_exp_gate, w_exp_up, w_exp_down)
    x_lat, x_ctx = x, ctx
    for l in range(depth):
        ctx_out = any(j % 2 == 0 for j in range(l + 1, depth))
        i = l // 2
        mod_l = mod[l]
        router = _router_weights(w_group[l], b_group[l], w_router[l], b_router[l])
        if l % 2 == 0:
            w_in = w_in_ab[i][:, in_cols].astype(BF16)
            w_out = w_out_ab[i].astype(BF16)
            wa, wb = w_out[out_rows], w_out[Q_W:]
            qg, kg = _pair_gain(q_gain[i]), _pair_gain(k_gain[i])
            bound = _score_bound(q_gain[i], k_gain[i])
            wf = _block_diag(w_fourier[i]).astype(BF16)
            bf = b_fourier[i].reshape(1, F_W)
            q, k, v, f = _ab_in_proj(x_lat, _vec(mod_l, 1, lat_rows), _vec(mod_l, 0, lat_rows),
                                     w_in, *rope_lat, qg, kg, tm=512)
            qc, kc, vc, fc = _ab_in_proj(x_ctx, _vec(mod_l, 1, ctx_rows), _vec(mod_l, 0, ctx_rows),
                                         w_in, *rope_ctx, qg, kg, tm=256)
            a_lat = _attention(q, [kc, k], [vc, v], bound, tq=256)
            cs, wc = dft[n]
            fm_lat = _fourier_mix(f, jnp.asarray(wc).astype(BF16), jnp.asarray(cs).astype(BF16), wf, bf, tr=512)
            ya_lat, yb_lat = a_lat, fm_lat
            if ctx_out:
                ya_ctx = _attention(qc, [kc], [vc], bound, tq=256)
                cs, wc = dft[lc]
                yb_ctx = _fourier_mix(fc, jnp.asarray(wc).astype(BF16), jnp.asarray(cs).astype(BF16), wf, bf, tr=256)
        else:
            w_in = w_in_cd[i].astype(BF16)
            w_out = w_out_cd[i].astype(BF16)
            wa, wb = w_out[:C_W], w_out[C_W:]
            wsp = w_spatial[i].astype(BF16)
            bsp = jnp.broadcast_to(b_spatial[i][:, :, None], (N_SGU_GROUPS, CHUNK, SGU_GROUP_DIM))
            cd = (conv_w[i], conv_b[i].reshape(1, D_W), conv_norm_g[i].reshape(1, D_W), conv_norm_b[i].reshape(1, D_W))
            ya_lat, glu = _cd_in_proj(x_lat, _vec(mod_l, 1, lat_rows), _vec(mod_l, 0, lat_rows),
                                      w_in, sgu_g[i], sgu_b[i], wsp, bsp, tm=256)
            yb_lat = _conv_module(glu, *cd, rows=256)
            if ctx_out:
                ya_ctx, glu_c = _cd_in_proj(x_ctx, _vec(mod_l, 1, ctx_rows), _vec(mod_l, 0, ctx_rows),
                                            w_in, sgu_g[i], sgu_b[i], wsp, bsp, tm=256)
                yb_ctx = _conv_module(glu_c, *cd, rows=256)
        x_lat = _channel_mix(ya_lat, yb_lat, wa, wb, x_lat, mod_l, lat_rows, ln_g[l], ln_b[l], router, experts, l, tm=512)
        if ctx_out:
            x_ctx = _channel_mix(ya_ctx, yb_ctx, wa, wb, x_ctx, mod_l, ctx_rows, ln_g[l], ln_b[l], router, experts, l, tm=256)
        else:
            x_ctx = None
    return x_lat
```

```python
import functools

import numpy as np
import jax
import jax.numpy as jnp
from jax import lax
from jax.experimental import pallas as pl
from jax.experimental.pallas import tpu as pltpu

F32 = jnp.float32
BF16 = jnp.bfloat16
U32 = jnp.uint32
HIGHEST = lax.Precision.HIGHEST

LANES = 128
D_MODEL = 1024
DEPTH = 4
GRID_W = 64
HEAD_DIM = 64
N_Q_HEADS = 12
N_KV_HEADS = 4
REP = N_Q_HEADS // N_KV_HEADS
ROPE_THETA = 10000.0
AXIS_DIM = HEAD_DIM // 2
N_FOURIER_GROUPS = 4
FOURIER_GROUP_DIM = 64
CHUNK = 128
N_SGU_GROUPS = 4
SGU_GROUP_DIM = 128
CONV_WIDTH = 31
CONV_HALO = 16
N_CONV_GROUPS = 4
CONV_GROUP_DIM = 128
N_EXPERT_GROUPS = 4
EXPERTS_PER_GROUP = 8
N_EXPERTS = N_EXPERT_GROUPS * EXPERTS_PER_GROUP
D_EXPERT = 256
Q_W = N_Q_HEADS * HEAD_DIM
KV_W = N_KV_HEADS * HEAD_DIM
F_W = N_FOURIER_GROUPS * FOURIER_GROUP_DIM
AB_IN = Q_W + 2 * KV_W + F_W
C_W = N_SGU_GROUPS * SGU_GROUP_DIM
D_W = N_CONV_GROUPS * CONV_GROUP_DIM
CD_IN = 2 * C_W + 2 * D_W
ALPHA = (2 * DEPTH) ** 0.25
EPS = 1e-6
N_Q_TILES = Q_W // LANES
N_KV_TILES = KV_W // LANES
LOG2E = 1.4426950408889634
Q_SCALE = LOG2E * HEAD_DIM ** -0.5
SCORE_BOUND_MAX = 60.0
ROUTER_LANES = LANES
TOP_K = 2
ROUTE_ID = 0
ROUTE_W = ROUTE_ID + TOP_K
EXPERT_TILE = 512
MOE_TOKEN_TILE = 512
ISSUE_UNROLL = 8
DMA_THREADS = 2
VMEM_LIMIT = 56 * 1024 * 1024


def _params(*sem):
    return pltpu.CompilerParams(dimension_semantics=sem, vmem_limit_bytes=VMEM_LIMIT)


def _silu(x):
    return x * jax.nn.sigmoid(x)


def _dot_split(x, w_hi, w_lo):
    x_hi = x.astype(BF16)
    x_lo = (x - x_hi.astype(F32)).astype(BF16)
    out = jnp.dot(x_hi, w_hi, preferred_element_type=F32)
    out = out + jnp.dot(x_lo, w_hi, preferred_element_type=F32)
    return out + jnp.dot(x_hi, w_lo, preferred_element_type=F32)


def _pack_bf16_pairs(x):
    half = x.shape[-1] // 2
    bits = lambda v: lax.bitcast_convert_type(v.astype(BF16).astype(F32), U32)
    return (bits(x[:, half:]) & U32(0xFFFF0000)) | lax.shift_right_logical(bits(x[:, :half]), U32(16))


def _unpack_bf16_pairs(w):
    lo = lax.bitcast_convert_type(lax.shift_left(w, U32(16)), F32)
    hi = lax.bitcast_convert_type(w & U32(0xFFFF0000), F32)
    return jnp.concatenate([lo, hi], axis=-1)


def _layer_norm(z, g, b):
    mu = jnp.mean(z, axis=-1, keepdims=True)
    zc = z - mu
    var = jnp.mean(zc * zc, axis=-1, keepdims=True)
    return zc * lax.rsqrt(var + EPS) * g + b


def _mod_kernel(s_ref, w_ref, b_ref, o_ref):
    s = _silu(s_ref[...])
    o_ref[0] = jnp.dot(s, w_ref[0], preferred_element_type=F32, precision=HIGHEST) + b_ref[0]


def _modulation(cond, w_mod, b_mod):
    n_layers, d, six_d = w_mod.shape
    rows = cond.shape[0]
    tn = 1536
    return pl.pallas_call(
        _mod_kernel,
        grid=(n_layers, six_d // tn),
        in_specs=[pl.BlockSpec((rows, d), lambda l, j: (0, 0)),
                  pl.BlockSpec((1, d, tn), lambda l, j: (l, 0, j)),
                  pl.BlockSpec((1, 1, tn), lambda l, j: (l, 0, j))],
        out_specs=pl.BlockSpec((1, rows, tn), lambda l, j: (l, 0, j)),
        out_shape=jax.ShapeDtypeStruct((n_layers, rows, six_d), F32),
        compiler_params=_params("parallel", "parallel"),
        name="modulation",
    )(cond, w_mod, b_mod.reshape(n_layers, 1, six_d))


def _ab_in_kernel(x_ref, sc_ref, sh_ref, w_ref, cos_ref, sin_ref, qg_ref, kg_ref,
                  q_ref, k_ref, v_ref, f_ref):
    tm = x_ref.shape[1]
    h = (x_ref[0] * (1.0 + sc_ref[0]) + sh_ref[0]).astype(BF16)
    p = jnp.dot(h, w_ref[...], preferred_element_type=F32)
    lane = lax.broadcasted_iota(jnp.int32, (tm, LANES), 1)
    low_head = lane < HEAD_DIM
    first_half = (lane % HEAD_DIM) < AXIS_DIM
    cos = cos_ref[...]
    sin = sin_ref[...]

    def norm_rope(t, gain):
        sq = t * t
        ss_lo = jnp.sum(jnp.where(low_head, sq, 0.0), axis=-1, keepdims=True)
        ss_hi = jnp.sum(jnp.where(low_head, 0.0, sq), axis=-1, keepdims=True)
        ms = jnp.where(low_head, ss_lo, ss_hi) * (1.0 / HEAD_DIM)
        tn = t * lax.rsqrt(ms + EPS) * gain
        partner = jnp.where(first_half,
                            pltpu.roll(tn, LANES - AXIS_DIM, 1),
                            pltpu.roll(tn, AXIS_DIM, 1))
        return tn * cos + partner * sin

    for j in range(N_Q_TILES):
        t = norm_rope(p[:, j * LANES:(j + 1) * LANES], qg_ref[...])
        q_ref[0, j] = (t * Q_SCALE).astype(BF16)
    for j in range(N_KV_TILES):
        c0 = Q_W + j * LANES
        k_ref[0, j] = norm_rope(p[:, c0:c0 + LANES], kg_ref[...]).astype(BF16)
        c1 = Q_W + KV_W + j * LANES
        v_ref[0, j] = p[:, c1:c1 + LANES].astype(BF16)
    f_ref[0] = p[:, Q_W + 2 * KV_W:].astype(BF16)


def _ab_in_proj(x, sc, sh, w_bf, cos, sin, qg, kg, tm):
    bsz, n, d = x.shape
    tm = min(tm, n)
    vec = pl.BlockSpec((1, 1, d), lambda b, i: (b, 0, 0))
    tab = pl.BlockSpec((tm, LANES), lambda b, i: (i, 0))
    gain = pl.BlockSpec((1, LANES), lambda b, i: (0, 0))
    return pl.pallas_call(
        _ab_in_kernel,
        grid=(bsz, n // tm),
        in_specs=[pl.BlockSpec((1, tm, d), lambda b, i: (b, i, 0)), vec, vec,
                  pl.BlockSpec((d, AB_IN), lambda b, i: (0, 0)), tab, tab, gain, gain],
        out_specs=[pl.BlockSpec((1, N_Q_TILES, tm, LANES), lambda b, i: (b, 0, i, 0)),
                   pl.BlockSpec((1, N_KV_TILES, tm, LANES), lambda b, i: (b, 0, i, 0)),
                   pl.BlockSpec((1, N_KV_TILES, tm, LANES), lambda b, i: (b, 0, i, 0)),
                   pl.BlockSpec((1, tm, F_W), lambda b, i: (b, i, 0))],
        out_shape=[jax.ShapeDtypeStruct((bsz, N_Q_TILES, n, LANES), BF16),
                   jax.ShapeDtypeStruct((bsz, N_KV_TILES, n, LANES), BF16),
                   jax.ShapeDtypeStruct((bsz, N_KV_TILES, n, LANES), BF16),
                   jax.ShapeDtypeStruct((bsz, n, F_W), BF16)],
        compiler_params=_params("parallel", "parallel"),
        name="ab_in_proj",
    )(x, sc, sh, w_bf, cos, sin, qg, kg)


def _attn_kernel(bound_ref, *refs, n_seg, row_max):
    q_ref = refs[0]
    k_refs = refs[1:1 + n_seg]
    v_refs = refs[1 + n_seg:1 + 2 * n_seg]
    o_ref = refs[1 + 2 * n_seg]
    tq = q_ref.shape[2]
    lane = lax.broadcasted_iota(jnp.int32, (tq, LANES), 1)
    low_head = lane < HEAD_DIM
    zero = jnp.zeros((), BF16)
    dn = (((1,), (1,)), ((), ()))
    for j in range(REP):
        q = q_ref[0, j]
        q2 = jnp.concatenate([jnp.where(low_head, q, zero), jnp.where(low_head, zero, q)], axis=0)
        s = [lax.dot_general(q2, k_ref[0, 0], dn, preferred_element_type=F32) for k_ref in k_refs]
        if row_max:
            m = s[0].max(axis=-1, keepdims=True)
            for si in s[1:]:
                m = jnp.maximum(m, si.max(axis=-1, keepdims=True))
        else:
            m = bound_ref[0]
        den = jnp.zeros((2 * tq, LANES), F32)
        o2 = jnp.zeros((2 * tq, LANES), F32)
        for si, v_ref in zip(s, v_refs):
            p = jnp.exp2(si - m)
            for t in range(si.shape[1] // LANES):
                den = den + p[:, t * LANES:(t + 1) * LANES]
            o2 = o2 + jnp.dot(p.astype(BF16), v_ref[0, 0], preferred_element_type=F32)
        o2 = o2 / jnp.sum(den, axis=-1, keepdims=True)
        o_ref[0, :, j * LANES:(j + 1) * LANES] = jnp.where(low_head, o2[:tq], o2[tq:]).astype(BF16)


def _attention_call(bound, q, ks, vs, tq, row_max):
    bsz, _, n, _ = q.shape
    tq = min(tq, n)
    n_seg = len(ks)
    kv_specs = [pl.BlockSpec((1, 1, a.shape[2], LANES), lambda b, i, t, bound: (b, t, 0, 0)) for a in ks + vs]
    return pl.pallas_call(
        functools.partial(_attn_kernel, n_seg=n_seg, row_max=row_max),
        grid_spec=pltpu.PrefetchScalarGridSpec(
            num_scalar_prefetch=1,
            grid=(bsz, n // tq, N_KV_TILES),
            in_specs=[pl.BlockSpec((1, REP, tq, LANES), lambda b, i, t, bound: (b, t, i, 0))] + kv_specs,
            out_specs=pl.BlockSpec((1, tq, REP * LANES), lambda b, i, t, bound: (b, i, t))),
        out_shape=jax.ShapeDtypeStruct((bsz, n, Q_W), BF16),
        compiler_params=_params("parallel", "parallel", "arbitrary"),
        name="attention_rowmax" if row_max else "attention",
    )(bound, q, *ks, *vs)


def _attention(q, ks, vs, bound, tq):
    n_seg = len(ks)

    def run(row_max):
        return lambda bound, q, *kv: _attention_call(bound, q, list(kv[:n_seg]), list(kv[n_seg:]), tq, row_max)

    return lax.cond(bound[0] <= SCORE_BOUND_MAX, run(False), run(True), bound, q, *ks, *vs)


def _fourier_kernel(f_ref, wc_ref, cs_ref, wf_ref, bf_ref, o_ref, y_ref):
    n = f_ref.shape[1]

    @pl.when(pl.program_id(1) == 0)
    def _():
        y = jnp.dot(f_ref[0], wc_ref[...], preferred_element_type=F32)
        y_ref[0:n, :] = y[:, :F_W].astype(BF16)
        y_ref[n:2 * n, :] = y[:, F_W:].astype(BF16)

    mixed = jnp.dot(cs_ref[...], y_ref[...], preferred_element_type=F32)
    out = jnp.dot(mixed.astype(BF16), wf_ref[...], preferred_element_type=F32) + bf_ref[...]
    o_ref[0] = out.astype(BF16)


def _fourier_mix(f, wc, cs, wf, bf, tr):
    bsz, n, _ = f.shape
    tr = min(tr, n)
    return pl.pallas_call(
        _fourier_kernel,
        grid=(bsz, n // tr),
        in_specs=[pl.BlockSpec((1, n, F_W), lambda b, i: (b, 0, 0)),
                  pl.BlockSpec((F_W, 2 * F_W), lambda b, i: (0, 0)),
                  pl.BlockSpec((tr, 2 * n), lambda b, i: (i, 0)),
                  pl.BlockSpec((F_W, F_W), lambda b, i: (0, 0)),
                  pl.BlockSpec((1, F_W), lambda b, i: (0, 0))],
        out_specs=pl.BlockSpec((1, tr, F_W), lambda b, i: (b, i, 0)),
        out_shape=jax.ShapeDtypeStruct((bsz, n, F_W), BF16),
        scratch_shapes=[pltpu.VMEM((2 * n, F_W), BF16)],
        compiler_params=_params("parallel", "arbitrary"),
        name="fourier_mix",
    )(f, wc, cs, wf, bf)


def _dft_tables(n):
    k = np.arange(n, dtype=np.int64)
    ang = 2.0 * np.pi * ((k[:, None] * k[None, :]) % n).astype(np.float64) / n
    scale = 1.0 / np.sqrt(float(n) * FOURIER_GROUP_DIM)
    cs = np.concatenate([np.cos(ang), -np.sin(ang)], axis=1) * scale
    c = np.arange(FOURIER_GROUP_DIM, dtype=np.int64)
    angc = 2.0 * np.pi * ((c[:, None] * c[None, :]) % FOURIER_GROUP_DIM).astype(np.float64) / FOURIER_GROUP_DIM
    eye = np.eye(N_FOURIER_GROUPS)
    wc = np.concatenate([np.kron(eye, np.cos(angc)), np.kron(eye, np.sin(angc))], axis=1)
    return cs.astype(np.float32), wc.astype(np.float32)


def _route(lg):
    lane = lax.broadcasted_iota(jnp.int32, lg.shape, 1).astype(F32)
    neg = jnp.float32(-1e30)
    first = lambda hit: jnp.min(jnp.where(hit, lane, float(LANES)), axis=-1, keepdims=True)
    gl = jnp.where(lane < N_EXPERT_GROUPS, lg, neg)
    gmax = gl.max(axis=-1, keepdims=True)
    g_idx = first(gl == gmax)
    g_w = 1.0 / jnp.sum(jnp.exp(gl - gmax), axis=-1, keepdims=True)
    lo = N_EXPERT_GROUPS + EXPERTS_PER_GROUP * g_idx
    el = jnp.where((lane >= lo) & (lane < lo + EXPERTS_PER_GROUP), lg, neg)
    v1 = el.max(axis=-1, keepdims=True)
    i1 = first(el == v1)
    el2 = jnp.where(lane == i1, neg, el)
    v2 = el2.max(axis=-1, keepdims=True)
    i2 = first(el2 == v2)
    e2 = jnp.exp(v2 - v1)
    w1 = g_w / (1.0 + e2)
    w2 = g_w * e2 / (1.0 + e2)
    out = jnp.where(lane == ROUTE_ID, i1 - N_EXPERT_GROUPS, 0.0)
    out = out + jnp.where(lane == ROUTE_ID + 1, i2 - N_EXPERT_GROUPS, 0.0)
    return out + jnp.where(lane == ROUTE_W, w1, 0.0) + jnp.where(lane == ROUTE_W + 1, w2, 0.0)


def _out_ln_route_kernel(a_ref, b_ref, wa_ref, wb_ref, x_ref, g1_ref, sc2_ref, sh2_ref,
                         lng_ref, lnb_ref, wrh_ref, wrl_ref, br_ref, x1_ref, h2_ref, route_ref):
    y = jnp.dot(a_ref[0], wa_ref[...], preferred_element_type=F32)
    y = y + jnp.dot(b_ref[0], wb_ref[...], preferred_element_type=F32)
    x1 = _layer_norm(ALPHA * x_ref[0] + g1_ref[0] * y, lng_ref[...], lnb_ref[...])
    x1_ref[0] = x1
    h2 = x1 * (1.0 + sc2_ref[0]) + sh2_ref[0]
    h2_ref[0] = _pack_bf16_pairs(h2)
    lg = _dot_split(h2, wrh_ref[...], wrl_ref[...]) + br_ref[...]
    route_ref[0] = _route(lg)


def _out_ln_route(a, b, wa, wb, x, g1, sc2, sh2, lng, lnb, wrh, wrl, br, tm):
    bsz, n, d = x.shape
    tm = min(tm, n)
    ka, kb = a.shape[-1], b.shape[-1]
    vec = pl.BlockSpec((1, 1, d), lambda bb, i: (bb, 0, 0))
    row = pl.BlockSpec((1, d), lambda bb, i: (0, 0))
    tile = lambda w: pl.BlockSpec((1, tm, w), lambda bb, i: (bb, i, 0))
    full = lambda r, c: pl.BlockSpec((r, c), lambda bb, i: (0, 0))
    return pl.pallas_call(
        _out_ln_route_kernel,
        grid=(bsz, n // tm),
        in_specs=[tile(ka), tile(kb), full(ka, d), full(kb, d), tile(d), vec, vec, vec, row, row,
                  full(d, ROUTER_LANES), full(d, ROUTER_LANES), full(1, ROUTER_LANES)],
        out_specs=[tile(d), tile(d // 2), tile(ROUTER_LANES)],
        out_shape=[jax.ShapeDtypeStruct((bsz, n, d), F32),
                   jax.ShapeDtypeStruct((bsz, n, d // 2), U32),
                   jax.ShapeDtypeStruct((bsz, n, ROUTER_LANES), F32)],
        compiler_params=_params("parallel", "parallel"),
        name="out_ln_route",
    )(a, b, wa, wb, x, g1, sc2, sh2, lng, lnb, wrh, wrl, br)


def _moe_plan(route, n_tiles):
    ids = route[:, ROUTE_ID:ROUTE_ID + TOP_K].astype(jnp.int32)
    flat = ids.T.reshape(-1)
    onehot = (flat[:, None] == jnp.arange(N_EXPERTS, dtype=jnp.int32)[None, :]).astype(jnp.int32)
    csum = jnp.cumsum(onehot, axis=0)
    counts = csum[-1]
    padded = ((counts + EXPERT_TILE - 1) // EXPERT_TILE) * EXPERT_TILE
    ends = jnp.cumsum(padded)
    starts = ends - padded
    pos = jnp.sum((csum - onehot + starts[None, :]) * onehot, axis=1)
    tile_start = jnp.arange(n_tiles, dtype=jnp.int32) * EXPERT_TILE
    tile_expert = jnp.minimum(jnp.sum((tile_start[:, None] >= ends[None, :]).astype(jnp.int32), axis=1),
                              N_EXPERTS - 1)
    used = (ends[-1] // EXPERT_TILE).reshape(1)
    return pos.reshape(TOP_K, -1), ends, tile_expert, used


def _row_copy(src, src_row, dst, dst_row, sem):
    return pltpu.make_async_copy(src.at[pl.ds(src_row, 1), :], dst.at[pl.ds(dst_row, 1), :], sem)


def _dispatch_kernel(ends_ref, pos_ref, h_ref, x_hbm, zero_ref, sem, zsem):
    tm = h_ref.shape[0]

    @pl.when(pl.program_id(0) == 0)
    def _():
        zero_ref[...] = jnp.zeros_like(zero_ref)
        for e in range(N_EXPERTS):
            end = ends_ref[e]
            begin = ends_ref[e - 1] if e else 0

            @pl.when(end > begin)
            def _():
                start = pl.multiple_of(end - EXPERT_TILE, EXPERT_TILE)
                cp = pltpu.make_async_copy(zero_ref, x_hbm.at[pl.ds(start, EXPERT_TILE), :], zsem)
                cp.start()
                cp.wait()

    def issue(i, carry):
        for u in range(ISSUE_UNROLL):
            r = i * ISSUE_UNROLL + u
            for k in range(TOP_K):
                _row_copy(h_ref, r, x_hbm, pos_ref[0, 0, k * tm + r], sem).start(priority=k % DMA_THREADS)
        return carry

    def drain(i, carry):
        for _ in range(ISSUE_UNROLL * TOP_K):
            _row_copy(h_ref, 0, x_hbm, 0, sem).wait()
        return carry

    lax.fori_loop(0, tm // ISSUE_UNROLL, issue, 0)
    lax.fori_loop(0, tm // ISSUE_UNROLL, drain, 0)


def _dispatch(h2, pos, ends, n_rows, tm):
    t, d = h2.shape
    pos_blocks = pos.reshape(TOP_K, t // tm, tm).transpose(1, 0, 2).reshape(t // tm, 1, TOP_K * tm)
    return pl.pallas_call(
        _dispatch_kernel,
        grid_spec=pltpu.PrefetchScalarGridSpec(
            num_scalar_prefetch=1,
            grid=(t // tm,),
            in_specs=[pl.BlockSpec((1, 1, TOP_K * tm), lambda i, ends: (i, 0, 0), memory_space=pltpu.SMEM),
                      pl.BlockSpec((tm, d), lambda i, ends: (i, 0))],
            out_specs=pl.BlockSpec(memory_space=pl.ANY),
            scratch_shapes=[pltpu.VMEM((EXPERT_TILE, d), h2.dtype),
                            pltpu.SemaphoreType.DMA, pltpu.SemaphoreType.DMA]),
        out_shape=jax.ShapeDtypeStruct((n_rows, d), h2.dtype),
        compiler_params=_params("arbitrary"),
        name="moe_dispatch",
    )(ends, pos_blocks, h2)


def _expert_kernel(te_ref, used_ref, x_ref, wg_ref, wu_ref, wd_ref, y_ref, wg_bf, wu_bf, wd_bf):
    j = pl.program_id(0)
    live = j < used_ref[0]
    new_expert = (j == 0) | (te_ref[j] != te_ref[jnp.maximum(j - 1, 0)])

    @pl.when(live & new_expert)
    def _():
        wg_bf[...] = wg_ref[0, 0].astype(BF16)
        wu_bf[...] = wu_ref[0, 0].astype(BF16)
        wd_bf[...] = wd_ref[0, 0].astype(BF16)

    @pl.when(live)
    def _():
        x = _unpack_bf16_pairs(x_ref[...]).astype(BF16)
        gate = jnp.dot(x, wg_bf[...], preferred_element_type=F32)
        up = jnp.dot(x, wu_bf[...], preferred_element_type=F32)
        hid = (_silu(gate) * up).astype(BF16)
        y_ref[...] = _pack_bf16_pairs(jnp.dot(hid, wd_bf[...], preferred_element_type=F32))

    @pl.when(jnp.logical_not(live))
    def _():
        y_ref[...] = jnp.zeros_like(y_ref)


def _experts(xs, tile_expert, used, wg, wu, wd, layer):
    n_rows, half = xs.shape
    d = 2 * half
    n_tiles = n_rows // EXPERT_TILE
    by_expert = lambda j, te, used: (layer, te[j], 0, 0)
    return pl.pallas_call(
        _expert_kernel,
        grid_spec=pltpu.PrefetchScalarGridSpec(
            num_scalar_prefetch=2,
            grid=(n_tiles,),
            in_specs=[pl.BlockSpec((EXPERT_TILE, half), lambda j, te, used: (jnp.minimum(j, used[0] - 1), 0)),
                      pl.BlockSpec((1, 1, d, D_EXPERT), by_expert),
                      pl.BlockSpec((1, 1, d, D_EXPERT), by_expert),
                      pl.BlockSpec((1, 1, D_EXPERT, d), by_expert)],
            out_specs=pl.BlockSpec((EXPERT_TILE, half), lambda j, te, used: (j, 0)),
            scratch_shapes=[pltpu.VMEM((d, D_EXPERT), BF16), pltpu.VMEM((d, D_EXPERT), BF16),
                            pltpu.VMEM((D_EXPERT, d), BF16)]),
        out_shape=jax.ShapeDtypeStruct((n_rows, half), U32),
        compiler_params=_params("arbitrary"),
        name="moe_experts",
    )(tile_expert, used, xs, wg, wu, wd)


def _combine_kernel(pos_ref, next_pos_ref, route_ref, x1_ref, g2_ref, lng_ref, lnb_ref, y_hbm, o_ref, ybuf, sems):
    tm = x1_ref.shape[0]
    i = pl.program_id(0)
    slot = i % 2

    def gather(idx_ref, s):
        def issue(it, carry):
            for u in range(ISSUE_UNROLL):
                r = it * ISSUE_UNROLL + u
                for k in range(TOP_K):
                    _row_copy(y_hbm, idx_ref[0, 0, k * tm + r], ybuf.at[s, k], r,
                              sems.at[s]).start(priority=k % DMA_THREADS)
            return carry

        lax.fori_loop(0, tm // ISSUE_UNROLL, issue, 0)

    @pl.when(i == 0)
    def _():
        gather(pos_ref, slot)

    @pl.when(i + 1 < pl.num_programs(0))
    def _():
        gather(next_pos_ref, 1 - slot)

    def drain(it, carry):
        for _ in range(ISSUE_UNROLL * TOP_K):
            _row_copy(y_hbm, 0, ybuf.at[slot, 0], 0, sems.at[slot]).wait()
        return carry

    lax.fori_loop(0, tm // ISSUE_UNROLL, drain, 0)

    route = route_ref[...]
    lane = lax.broadcasted_iota(jnp.int32, route.shape, 1)
    y = jnp.zeros_like(o_ref)
    for k in range(TOP_K):
        w = jnp.sum(jnp.where(lane == ROUTE_W + k, route, 0.0), axis=-1, keepdims=True)
        y = y + w * _unpack_bf16_pairs(ybuf[slot, k])
    z = ALPHA * x1_ref[...] + g2_ref[0] * y
    o_ref[...] = _layer_norm(z, lng_ref[...], lnb_ref[...])


def _combine(y, pos, route, x1, g2, lng, lnb, tiles_per_batch, tm):
    t, d = x1.shape
    steps = t // tm
    pos_blocks = pos.reshape(TOP_K, steps, tm).transpose(1, 0, 2).reshape(steps, 1, TOP_K * tm)
    return pl.pallas_call(
        _combine_kernel,
        grid=(steps,),
        in_specs=[pl.BlockSpec((1, 1, TOP_K * tm), lambda i: (i, 0, 0), memory_space=pltpu.SMEM),
                  pl.BlockSpec((1, 1, TOP_K * tm), lambda i: (jnp.minimum(i + 1, steps - 1), 0, 0),
                               memory_space=pltpu.SMEM),
                  pl.BlockSpec((tm, ROUTER_LANES), lambda i: (i, 0)),
                  pl.BlockSpec((tm, d), lambda i: (i, 0)),
                  pl.BlockSpec((1, 1, d), lambda i: (i // tiles_per_batch, 0, 0)),
                  pl.BlockSpec((1, d), lambda i: (0, 0)),
                  pl.BlockSpec((1, d), lambda i: (0, 0)),
                  pl.BlockSpec(memory_space=pl.ANY)],
        out_specs=pl.BlockSpec((tm, d), lambda i: (i, 0)),
        out_shape=jax.ShapeDtypeStruct((t, d), F32),
        scratch_shapes=[pltpu.VMEM((2, TOP_K, tm, d // 2), U32), pltpu.SemaphoreType.DMA((2,))],
        compiler_params=_params("arbitrary"),
        name="moe_combine",
    )(pos_blocks, pos_blocks, route, x1, g2, lng, lnb, y)


def _moe(h2, route, experts, layer, x1, g2, lng, lnb):
    bsz, n, d = x1.shape
    t = bsz * n
    tm = min(MOE_TOKEN_TILE, n)
    n_tiles = TOP_K * t // EXPERT_TILE + N_EXPERTS
    route = route.reshape(t, ROUTER_LANES)
    pos, ends, tile_expert, used = _moe_plan(route, n_tiles)
    xs = _dispatch(h2.reshape(t, d // 2), pos, ends, n_tiles * EXPERT_TILE, tm)
    y = _experts(xs, tile_expert, used, *experts, layer)
    out = _combine(y, pos, route, x1.reshape(t, d), g2, lng, lnb, n // tm, tm)
    return out.reshape(bsz, n, d)


def _gelu_tanh(x):
    return 0.5 * x * (1.0 + jnp.tanh(np.sqrt(2.0 / np.pi).astype(np.float32) * (x + 0.044715 * (x * x * x))))


def _cd_in_kernel(x_ref, sc_ref, sh_ref, w_ref, sg_ref, sb_ref, wsp_ref, bsp_ref, yc_ref, glu_ref):
    tm = x_ref.shape[1]
    h = (x_ref[0] * (1.0 + sc_ref[0]) + sh_ref[0]).astype(BF16)
    p = jnp.dot(h, w_ref[...], preferred_element_type=F32)
    for g in range(N_SGU_GROUPS):
        u = _gelu_tanh(p[:, g * SGU_GROUP_DIM:(g + 1) * SGU_GROUP_DIM])
        v = _gelu_tanh(p[:, C_W + g * SGU_GROUP_DIM:C_W + (g + 1) * SGU_GROUP_DIM])
        vg = _layer_norm(v, sg_ref[g:g + 1, :], sb_ref[g:g + 1, :]).astype(BF16)
        for c in range(tm // CHUNK):
            rows = slice(c * CHUNK, (c + 1) * CHUNK)
            sv = jnp.dot(wsp_ref[g], vg[rows], preferred_element_type=F32) + bsp_ref[g]
            yc_ref[0, rows, g * SGU_GROUP_DIM:(g + 1) * SGU_GROUP_DIM] = (u[rows] * sv).astype(BF16)
    a = p[:, 2 * C_W:2 * C_W + D_W]
    gate = p[:, 2 * C_W + D_W:]
    glu_ref[0] = a * jax.nn.sigmoid(gate)


def _cd_in_proj(x, sc, sh, w_bf, sg, sb, wsp_bf, bsp, tm):
    bsz, n, d = x.shape
    tm = min(tm, n)
    vec = pl.BlockSpec((1, 1, d), lambda b, i: (b, 0, 0))
    return pl.pallas_call(
        _cd_in_kernel,
        grid=(bsz, n // tm),
        in_specs=[pl.BlockSpec((1, tm, d), lambda b, i: (b, i, 0)), vec, vec,
                  pl.BlockSpec((d, CD_IN), lambda b, i: (0, 0)),
                  pl.BlockSpec((N_SGU_GROUPS, SGU_GROUP_DIM), lambda b, i: (0, 0)),
                  pl.BlockSpec((N_SGU_GROUPS, SGU_GROUP_DIM), lambda b, i: (0, 0)),
                  pl.BlockSpec((N_SGU_GROUPS, CHUNK, CHUNK), lambda b, i: (0, 0, 0)),
                  pl.BlockSpec((N_SGU_GROUPS, CHUNK, SGU_GROUP_DIM), lambda b, i: (0, 0, 0))],
        out_specs=[pl.BlockSpec((1, tm, C_W), lambda b, i: (b, i, 0)),
                   pl.BlockSpec((1, tm, D_W), lambda b, i: (b, i, 0))],
        out_shape=[jax.ShapeDtypeStruct((bsz, n, C_W), BF16),
                   jax.ShapeDtypeStruct((bsz, n, D_W), F32)],
        compiler_params=_params("parallel", "parallel"),
        name="cd_in_proj",
    )(x, sc, sh, w_bf, sg, sb, wsp_bf, bsp)


def _conv_kernel(glu_ref, w_ref, cb_ref, g_ref, b_ref, o_ref, pad_ref, *, rows):
    n = glu_ref.shape[1]
    halo = jnp.zeros((CONV_HALO, CONV_GROUP_DIM), F32)
    pad_ref[0:CONV_HALO, :] = halo
    pad_ref[CONV_HALO + n:CONV_HALO + n + CONV_HALO, :] = halo
    pad_ref[CONV_HALO:CONV_HALO + n, :] = glu_ref[0]
    first = CONV_HALO - CONV_WIDTH // 2
    for c in range(n // rows):
        acc = jnp.zeros((rows, CONV_GROUP_DIM), F32) + cb_ref[...]
        for k in range(CONV_WIDTH):
            start = c * rows + first + k
            acc = acc + w_ref[k:k + 1, :] * pad_ref[start:start + rows, :]
        dn = _layer_norm(acc, g_ref[...], b_ref[...])
        o_ref[0, c * rows:(c + 1) * rows, :] = _silu(dn).astype(BF16)


def _conv_module(glu, conv_w, conv_b, cn_g, cn_b, rows):
    bsz, n, _ = glu.shape
    rows = min(rows, n)
    vec = pl.BlockSpec((1, CONV_GROUP_DIM), lambda b, g: (0, g))
    return pl.pallas_call(
        functools.partial(_conv_kernel, rows=rows),
        grid=(bsz, N_CONV_GROUPS),
        in_specs=[pl.BlockSpec((1, n, CONV_GROUP_DIM), lambda b, g: (b, 0, g)),
                  pl.BlockSpec((CONV_WIDTH, CONV_GROUP_DIM), lambda b, g: (0, g)),
                  vec, vec, vec],
        out_specs=pl.BlockSpec((1, n, CONV_GROUP_DIM), lambda b, g: (b, 0, g)),
        out_shape=jax.ShapeDtypeStruct((bsz, n, D_W), BF16),
        scratch_shapes=[pltpu.VMEM((n + 2 * CONV_HALO, CONV_GROUP_DIM), F32)],
        compiler_params=_params("parallel", "parallel"),
        name="conv_module",
    )(glu, conv_w, conv_b, cn_g, cn_b)


def _head_pair_columns():
    inner = np.concatenate([np.arange(0, HEAD_DIM, 2), np.arange(1, HEAD_DIM, 2)])
    cols = []
    for t in range(N_KV_TILES):
        for r in range(REP):
            for g in (2 * t, 2 * t + 1):
                cols.append((g * REP + r) * HEAD_DIM + inner)
    for g in range(N_KV_HEADS):
        cols.append(Q_W + g * HEAD_DIM + inner)
    cols.append(np.arange(Q_W + KV_W, AB_IN))
    return np.concatenate(cols)


def _attn_out_rows():
    rows = []
    for t in range(N_KV_TILES):
        for r in range(REP):
            for g in (2 * t, 2 * t + 1):
                rows.append((g * REP + r) * HEAD_DIM + np.arange(HEAD_DIM))
    return np.concatenate(rows)


def _rope_tables(n, rotate):
    if not rotate:
        return jnp.ones((n, LANES), F32), jnp.zeros((n, LANES), F32)
    rows = n // GRID_W
    row = jnp.repeat(jnp.arange(rows, dtype=F32), GRID_W)
    col = jnp.tile(jnp.arange(GRID_W, dtype=F32), rows)
    inv_freq = 1.0 / (ROPE_THETA ** (jnp.arange(0, AXIS_DIM, 2, dtype=F32) / AXIS_DIM))
    ang = jnp.concatenate([row[:, None] * inv_freq, col[:, None] * inv_freq], axis=-1)
    cos, sin = jnp.cos(ang), jnp.sin(ang)
    return jnp.tile(cos, (1, 4)), jnp.tile(jnp.concatenate([-sin, sin], axis=-1), (1, 2))


def _pair_gain(g):
    perm = np.concatenate([np.arange(0, HEAD_DIM, 2), np.arange(1, HEAD_DIM, 2)])
    return jnp.tile(g[perm], 2).reshape(1, LANES)


def _score_bound(qg, kg):
    bound = 1.02 * HEAD_DIM * Q_SCALE * jnp.max(jnp.abs(qg)) * jnp.max(jnp.abs(kg))
    return bound.reshape(1).astype(F32)


def _block_diag(w):
    g, a, b = w.shape
    out = jnp.zeros((g * a, g * b), w.dtype)
    for i in range(g):
        out = out.at[i * a:(i + 1) * a, i * b:(i + 1) * b].set(w[i])
    return out


def _router_weights(w_group, b_group, w_router, b_router):
    d = w_group.shape[0]
    wr = jnp.concatenate([w_group, jnp.transpose(w_router, (1, 0, 2)).reshape(d, N_EXPERTS)], axis=1)
    br = jnp.concatenate([b_group, b_router.reshape(N_EXPERTS)])
    pad = ROUTER_LANES - wr.shape[1]
    wr = jnp.pad(wr, ((0, 0), (0, pad)))
    wr_hi = wr.astype(BF16)
    wr_lo = (wr - wr_hi.astype(F32)).astype(BF16)
    return wr_hi, wr_lo, jnp.pad(br, (0, pad)).reshape(1, ROUTER_LANES)


def _vec(mod_l, k, rows):
    first, count, bsz = rows
    v = mod_l[first:first + count, k * D_MODEL:(k + 1) * D_MODEL][:, None, :]
    return jnp.broadcast_to(v, (bsz, 1, D_MODEL))


def _channel_mix(y_a, y_b, wa, wb, x, mod_l, rows, ln_g, ln_b, router, experts, layer, tm):
    lng = lambda i: ln_g[i].reshape(1, D_MODEL)
    lnb = lambda i: ln_b[i].reshape(1, D_MODEL)
    x1, h2, route = _out_ln_route(y_a, y_b, wa, wb, x, _vec(mod_l, 2, rows), _vec(mod_l, 4, rows),
                                  _vec(mod_l, 3, rows), lng(0), lnb(0), *router, tm=tm)
    return _moe(h2, route, experts, layer, x1, _vec(mod_l, 5, rows), lng(1), lnb(1))


def kernel(x, c, ctx, c_ctx, w_mod, b_mod, ln_g, ln_b, w_in_ab, w_out_ab, q_gain, k_gain, w_fourier, b_fourier, w_in_cd, w_out_cd, sgu_g, sgu_b, w_spatial, b_spatial, conv_w, conv_b, conv_norm_g, conv_norm_b, w_group, b_group, w_router, b_router, w_exp_gate, w_exp_up, w_exp_down):
    bsz, n, _ = x.shape
    lc = ctx.shape[1]
    depth = w_mod.shape[0]
    pad_rows = (-(bsz + 1)) % 8
    cond = jnp.concatenate([c, c_ctx[None, :], jnp.zeros((pad_rows, D_MODEL), F32)], axis=0)
    mod = _modulation(cond, w_mod, b_mod)
    lat_rows = (0, bsz, bsz)
    ctx_rows = (bsz, 1, bsz)

    in_cols = _head_pair_columns()
    out_rows = _attn_out_rows()
    rope_lat = _rope_tables(n, True)
    rope_ctx = _rope_tables(lc, False)
    dft = {m: _dft_tables(m) for m in {n, lc}}

    experts = (w_exp_gate, w_exp_up, w_exp_down)
    x_lat, x_ctx = x, ctx
    for l in range(depth):
        ctx_out = any(j % 2 == 0 for j in range(l + 1, depth))
        i = l // 2
        mod_l = mod[l]
        router = _router_weights(w_group[l], b_group[l], w_router[l], b_router[l])
        if l % 2 == 0:
            w_in = w_in_ab[i][:, in_cols].astype(BF16)
            w_out = w_out_ab[i].astype(BF16)
            wa, wb = w_out[out_rows], w_out[Q_W:]
            qg, kg = _pair_gain(q_gain[i]), _pair_gain(k_gain[i])
            bound = _score_bound(q_gain[i], k_gain[i])
            wf = _block_diag(w_fourier[i]).astype(BF16)
            bf = b_fourier[i].reshape(1, F_W)
            q, k, v, f = _ab_in_proj(x_lat, _vec(mod_l, 1, lat_rows), _vec(mod_l, 0, lat_rows),
                                     w_in, *rope_lat, qg, kg, tm=512)
            qc, kc, vc, fc = _ab_in_proj(x_ctx, _vec(mod_l, 1, ctx_rows), _vec(mod_l, 0, ctx_rows),
                                         w_in, *rope_ctx, qg, kg, tm=256)
            a_lat = _attention(q, [kc, k], [vc, v], bound, tq=512)
            cs, wc = dft[n]
            fm_lat = _fourier_mix(f, jnp.asarray(wc).astype(BF16), jnp.asarray(cs).astype(BF16), wf, bf, tr=512)
            ya_lat, yb_lat = a_lat, fm_lat
            if ctx_out:
                ya_ctx = _attention(qc, [kc], [vc], bound, tq=256)
                cs, wc = dft[lc]
                yb_ctx = _fourier_mix(fc, jnp.asarray(wc).astype(BF16), jnp.asarray(cs).astype(BF16), wf, bf, tr=256)
        else:
            w_in = w_in_cd[i].astype(BF16)
            w_out = w_out_cd[i].astype(BF16)
            wa, wb = w_out[:C_W], w_out[C_W:]
            wsp = w_spatial[i].astype(BF16)
            bsp = jnp.broadcast_to(b_spatial[i][:, :, None], (N_SGU_GROUPS, CHUNK, SGU_GROUP_DIM))
            cd = (conv_w[i], conv_b[i].reshape(1, D_W), conv_norm_g[i].reshape(1, D_W), conv_norm_b[i].reshape(1, D_W))
            ya_lat, glu = _cd_in_proj(x_lat, _vec(mod_l, 1, lat_rows), _vec(mod_l, 0, lat_rows),
                                      w_in, sgu_g[i], sgu_b[i], wsp, bsp, tm=256)
            yb_lat = _conv_module(glu, *cd, rows=256)
            if ctx_out:
                ya_ctx, glu_c = _cd_in_proj(x_ctx, _vec(mod_l, 1, ctx_rows), _vec(mod_l, 0, ctx_rows),
                                            w_in, sgu_g[i], sgu_b[i], wsp, bsp, tm=256)
                yb_ctx = _conv_module(glu_c, *cd, rows=256)
        x_lat = _channel_mix(ya_lat, yb_lat, wa, wb, x_lat, mod_l, lat_rows, ln_g[l], ln_b[l], router, experts, l, tm=512)
        if ctx_out:
            x_ctx = _channel_mix(ya_ctx, yb_ctx, wa, wb, x_ctx, mod_l, ctx_rows, ln_g[l], ln_b[l], router, experts, l, tm=256)
        else:
            x_ctx = None
    return x_lat
```

```python
import functools

import numpy as np
import jax
import jax.numpy as jnp
from jax import lax
from jax.experimental import pallas as pl
from jax.experimental.pallas import tpu as pltpu

F32 = jnp.float32
BF16 = jnp.bfloat16
U32 = jnp.uint32
HIGHEST = lax.Precision.HIGHEST

LANES = 128
D_MODEL = 1024
DEPTH = 4
GRID_W = 64
HEAD_DIM = 64
N_Q_HEADS = 12
N_KV_HEADS = 4
REP = N_Q_HEADS // N_KV_HEADS
ROPE_THETA = 10000.0
AXIS_DIM = HEAD_DIM // 2
N_FOURIER_GROUPS = 4
FOURIER_GROUP_DIM = 64
CHUNK = 128
N_SGU_GROUPS = 4
SGU_GROUP_DIM = 128
CONV_WIDTH = 31
CONV_HALO = 16
N_CONV_GROUPS = 4
CONV_GROUP_DIM = 128
N_EXPERT_GROUPS = 4
EXPERTS_PER_GROUP = 8
N_EXPERTS = N_EXPERT_GROUPS * EXPERTS_PER_GROUP
D_EXPERT = 256
Q_W = N_Q_HEADS * HEAD_DIM
KV_W = N_KV_HEADS * HEAD_DIM
F_W = N_FOURIER_GROUPS * FOURIER_GROUP_DIM
AB_IN = Q_W + 2 * KV_W + F_W
C_W = N_SGU_GROUPS * SGU_GROUP_DIM
D_W = N_CONV_GROUPS * CONV_GROUP_DIM
CD_IN = 2 * C_W + 2 * D_W
ALPHA = (2 * DEPTH) ** 0.25
EPS = 1e-6
N_Q_TILES = Q_W // LANES
N_KV_TILES = KV_W // LANES
LOG2E = 1.4426950408889634
Q_SCALE = LOG2E * HEAD_DIM ** -0.5
SCORE_BOUND_MAX = 60.0
ROUTER_LANES = LANES
TOP_K = 2
ROUTE_ID = 0
ROUTE_W = ROUTE_ID + TOP_K
EXPERT_TILE = 512
MOE_TOKEN_TILE = 512
ISSUE_UNROLL = 8
COMBINE_ROWS = 128
DMA_THREADS = 2
VMEM_LIMIT = 56 * 1024 * 1024


def _params(*sem):
    return pltpu.CompilerParams(dimension_semantics=sem, vmem_limit_bytes=VMEM_LIMIT)


def _silu(x):
    return x * jax.nn.sigmoid(x)


def _dot_split(x, w_hi, w_lo):
    x_hi = x.astype(BF16)
    x_lo = (x - x_hi.astype(F32)).astype(BF16)
    out = jnp.dot(x_hi, w_hi, preferred_element_type=F32)
    out = out + jnp.dot(x_lo, w_hi, preferred_element_type=F32)
    return out + jnp.dot(x_hi, w_lo, preferred_element_type=F32)


def _pack_bf16_pairs(x):
    half = x.shape[-1] // 2
    bits = lambda v: lax.bitcast_convert_type(v.astype(BF16).astype(F32), U32)
    return (bits(x[:, half:]) & U32(0xFFFF0000)) | lax.shift_right_logical(bits(x[:, :half]), U32(16))


def _unpack_bf16_pairs(w):
    lo = lax.bitcast_convert_type(lax.shift_left(w, U32(16)), F32)
    hi = lax.bitcast_convert_type(w & U32(0xFFFF0000), F32)
    return jnp.concatenate([lo, hi], axis=-1)


def _layer_norm(z, g, b):
    mu = jnp.mean(z, axis=-1, keepdims=True)
    zc = z - mu
    var = jnp.mean(zc * zc, axis=-1, keepdims=True)
    return zc * lax.rsqrt(var + EPS) * g + b


def _mod_kernel(s_ref, w_ref, b_ref, o_ref):
    s = _silu(s_ref[...])
    o_ref[0] = jnp.dot(s, w_ref[0], preferred_element_type=F32, precision=HIGHEST) + b_ref[0]


def _modulation(cond, w_mod, b_mod):
    n_layers, d, six_d = w_mod.shape
    rows = cond.shape[0]
    tn = 1536
    return pl.pallas_call(
        _mod_kernel,
        grid=(n_layers, six_d // tn),
        in_specs=[pl.BlockSpec((rows, d), lambda l, j: (0, 0)),
                  pl.BlockSpec((1, d, tn), lambda l, j: (l, 0, j)),
                  pl.BlockSpec((1, 1, tn), lambda l, j: (l, 0, j))],
        out_specs=pl.BlockSpec((1, rows, tn), lambda l, j: (l, 0, j)),
        out_shape=jax.ShapeDtypeStruct((n_layers, rows, six_d), F32),
        compiler_params=_params("parallel", "parallel"),
        name="modulation",
    )(cond, w_mod, b_mod.reshape(n_layers, 1, six_d))


def _ab_in_kernel(x_ref, sc_ref, sh_ref, w_ref, cos_ref, sin_ref, qg_ref, kg_ref,
                  q_ref, k_ref, v_ref, f_ref):
    tm = x_ref.shape[1]
    h = (x_ref[0] * (1.0 + sc_ref[0]) + sh_ref[0]).astype(BF16)
    p = jnp.dot(h, w_ref[...], preferred_element_type=F32)
    lane = lax.broadcasted_iota(jnp.int32, (tm, LANES), 1)
    low_head = lane < HEAD_DIM
    first_half = (lane % HEAD_DIM) < AXIS_DIM
    cos = cos_ref[...]
    sin = sin_ref[...]

    def norm_rope(t, gain):
        sq = t * t
        ss_lo = jnp.sum(jnp.where(low_head, sq, 0.0), axis=-1, keepdims=True)
        ss_hi = jnp.sum(jnp.where(low_head, 0.0, sq), axis=-1, keepdims=True)
        ms = jnp.where(low_head, ss_lo, ss_hi) * (1.0 / HEAD_DIM)
        tn = t * lax.rsqrt(ms + EPS) * gain
        partner = jnp.where(first_half,
                            pltpu.roll(tn, LANES - AXIS_DIM, 1),
                            pltpu.roll(tn, AXIS_DIM, 1))
        return tn * cos + partner * sin

    for j in range(N_Q_TILES):
        t = norm_rope(p[:, j * LANES:(j + 1) * LANES], qg_ref[...])
        q_ref[0, j] = (t * Q_SCALE).astype(BF16)
    for j in range(N_KV_TILES):
        c0 = Q_W + j * LANES
        k_ref[0, j] = norm_rope(p[:, c0:c0 + LANES], kg_ref[...]).astype(BF16)
        c1 = Q_W + KV_W + j * LANES
        v_ref[0, j] = p[:, c1:c1 + LANES].astype(BF16)
    f_ref[0] = p[:, Q_W + 2 * KV_W:].astype(BF16)


def _ab_in_proj(x, sc, sh, w_bf, cos, sin, qg, kg, tm):
    bsz, n, d = x.shape
    tm = min(tm, n)
    vec = pl.BlockSpec((1, 1, d), lambda b, i: (b, 0, 0))
    tab = pl.BlockSpec((tm, LANES), lambda b, i: (i, 0))
    gain = pl.BlockSpec((1, LANES), lambda b, i: (0, 0))
    return pl.pallas_call(
        _ab_in_kernel,
        grid=(bsz, n // tm),
        in_specs=[pl.BlockSpec((1, tm, d), lambda b, i: (b, i, 0)), vec, vec,
                  pl.BlockSpec((d, AB_IN), lambda b, i: (0, 0)), tab, tab, gain, gain],
        out_specs=[pl.BlockSpec((1, N_Q_TILES, tm, LANES), lambda b, i: (b, 0, i, 0)),
                   pl.BlockSpec((1, N_KV_TILES, tm, LANES), lambda b, i: (b, 0, i, 0)),
                   pl.BlockSpec((1, N_KV_TILES, tm, LANES), lambda b, i: (b, 0, i, 0)),
                   pl.BlockSpec((1, tm, F_W), lambda b, i: (b, i, 0))],
        out_shape=[jax.ShapeDtypeStruct((bsz, N_Q_TILES, n, LANES), BF16),
                   jax.ShapeDtypeStruct((bsz, N_KV_TILES, n, LANES), BF16),
                   jax.ShapeDtypeStruct((bsz, N_KV_TILES, n, LANES), BF16),
                   jax.ShapeDtypeStruct((bsz, n, F_W), BF16)],
        compiler_params=_params("parallel", "parallel"),
        name="ab_in_proj",
    )(x, sc, sh, w_bf, cos, sin, qg, kg)


def _attn_kernel(bound_ref, *refs, n_seg, row_max):
    q_ref = refs[0]
    k_refs = refs[1:1 + n_seg]
    v_refs = refs[1 + n_seg:1 + 2 * n_seg]
    o_ref = refs[1 + 2 * n_seg]
    tq = q_ref.shape[2]
    lane = lax.broadcasted_iota(jnp.int32, (tq, LANES), 1)
    low_head = lane < HEAD_DIM
    zero = jnp.zeros((), BF16)
    dn = (((1,), (1,)), ((), ()))
    for j in range(REP):
        q = q_ref[0, j]
        q2 = jnp.concatenate([jnp.where(low_head, q, zero), jnp.where(low_head, zero, q)], axis=0)
        s = [lax.dot_general(q2, k_ref[0, 0], dn, preferred_element_type=F32) for k_ref in k_refs]
        if row_max:
            m = s[0].max(axis=-1, keepdims=True)
            for si in s[1:]:
                m = jnp.maximum(m, si.max(axis=-1, keepdims=True))
        else:
            m = bound_ref[0]
        den = jnp.zeros((2 * tq, LANES), F32)
        o2 = jnp.zeros((2 * tq, LANES), F32)
        for si, v_ref in zip(s, v_refs):
            p = jnp.exp2(si - m)
            for t in range(si.shape[1] // LANES):
                den = den + p[:, t * LANES:(t + 1) * LANES]
            o2 = o2 + jnp.dot(p.astype(BF16), v_ref[0, 0], preferred_element_type=F32)
        o2 = o2 / jnp.sum(den, axis=-1, keepdims=True)
        o_ref[0, :, j * LANES:(j + 1) * LANES] = jnp.where(low_head, o2[:tq], o2[tq:]).astype(BF16)


def _attention_call(bound, q, ks, vs, tq, row_max):
    bsz, _, n, _ = q.shape
    tq = min(tq, n)
    n_seg = len(ks)
    kv_specs = [pl.BlockSpec((1, 1, a.shape[2], LANES), lambda b, i, t, bound: (b, t, 0, 0)) for a in ks + vs]
    return pl.pallas_call(
        functools.partial(_attn_kernel, n_seg=n_seg, row_max=row_max),
        grid_spec=pltpu.PrefetchScalarGridSpec(
            num_scalar_prefetch=1,
            grid=(bsz, n // tq, N_KV_TILES),
            in_specs=[pl.BlockSpec((1, REP, tq, LANES), lambda b, i, t, bound: (b, t, i, 0))] + kv_specs,
            out_specs=pl.BlockSpec((1, tq, REP * LANES), lambda b, i, t, bound: (b, i, t))),
        out_shape=jax.ShapeDtypeStruct((bsz, n, Q_W), BF16),
        compiler_params=_params("parallel", "parallel", "arbitrary"),
        name="attention_rowmax" if row_max else "attention",
    )(bound, q, *ks, *vs)


def _attention(q, ks, vs, bound, tq):
    n_seg = len(ks)

    def run(row_max):
        return lambda bound, q, *kv: _attention_call(bound, q, list(kv[:n_seg]), list(kv[n_seg:]), tq, row_max)

    return lax.cond(bound[0] <= SCORE_BOUND_MAX, run(False), run(True), bound, q, *ks, *vs)


def _fourier_kernel(f_ref, wc_ref, cs_ref, wf_ref, bf_ref, o_ref, y_ref):
    n = f_ref.shape[1]

    @pl.when(pl.program_id(1) == 0)
    def _():
        y = jnp.dot(f_ref[0], wc_ref[...], preferred_element_type=F32)
        y_ref[0:n, :] = y[:, :F_W].astype(BF16)
        y_ref[n:2 * n, :] = y[:, F_W:].astype(BF16)

    mixed = jnp.dot(cs_ref[...], y_ref[...], preferred_element_type=F32)
    out = jnp.dot(mixed.astype(BF16), wf_ref[...], preferred_element_type=F32) + bf_ref[...]
    o_ref[0] = out.astype(BF16)


def _fourier_mix(f, wc, cs, wf, bf, tr):
    bsz, n, _ = f.shape
    tr = min(tr, n)
    return pl.pallas_call(
        _fourier_kernel,
        grid=(bsz, n // tr),
        in_specs=[pl.BlockSpec((1, n, F_W), lambda b, i: (b, 0, 0)),
                  pl.BlockSpec((F_W, 2 * F_W), lambda b, i: (0, 0)),
                  pl.BlockSpec((tr, 2 * n), lambda b, i: (i, 0)),
                  pl.BlockSpec((F_W, F_W), lambda b, i: (0, 0)),
                  pl.BlockSpec((1, F_W), lambda b, i: (0, 0))],
        out_specs=pl.BlockSpec((1, tr, F_W), lambda b, i: (b, i, 0)),
        out_shape=jax.ShapeDtypeStruct((bsz, n, F_W), BF16),
        scratch_shapes=[pltpu.VMEM((2 * n, F_W), BF16)],
        compiler_params=_params("parallel", "arbitrary"),
        name="fourier_mix",
    )(f, wc, cs, wf, bf)


def _dft_tables(n):
    k = np.arange(n, dtype=np.int64)
    ang = 2.0 * np.pi * ((k[:, None] * k[None, :]) % n).astype(np.float64) / n
    scale = 1.0 / np.sqrt(float(n) * FOURIER_GROUP_DIM)
    cs = np.concatenate([np.cos(ang), -np.sin(ang)], axis=1) * scale
    c = np.arange(FOURIER_GROUP_DIM, dtype=np.int64)
    angc = 2.0 * np.pi * ((c[:, None] * c[None, :]) % FOURIER_GROUP_DIM).astype(np.float64) / FOURIER_GROUP_DIM
    eye = np.eye(N_FOURIER_GROUPS)
    wc = np.concatenate([np.kron(eye, np.cos(angc)), np.kron(eye, np.sin(angc))], axis=1)
    return cs.astype(np.float32), wc.astype(np.float32)


def _route(lg):
    lane = lax.broadcasted_iota(jnp.int32, lg.shape, 1).astype(F32)
    neg = jnp.float32(-1e30)
    first = lambda hit: jnp.min(jnp.where(hit, lane, float(LANES)), axis=-1, keepdims=True)
    gl = jnp.where(lane < N_EXPERT_GROUPS, lg, neg)
    gmax = gl.max(axis=-1, keepdims=True)
    g_idx = first(gl == gmax)
    g_w = 1.0 / jnp.sum(jnp.exp(gl - gmax), axis=-1, keepdims=True)
    lo = N_EXPERT_GROUPS + EXPERTS_PER_GROUP * g_idx
    el = jnp.where((lane >= lo) & (lane < lo + EXPERTS_PER_GROUP), lg, neg)
    v1 = el.max(axis=-1, keepdims=True)
    i1 = first(el == v1)
    el2 = jnp.where(lane == i1, neg, el)
    v2 = el2.max(axis=-1, keepdims=True)
    i2 = first(el2 == v2)
    e2 = jnp.exp(v2 - v1)
    w1 = g_w / (1.0 + e2)
    w2 = g_w * e2 / (1.0 + e2)
    out = jnp.where(lane == ROUTE_ID, i1 - N_EXPERT_GROUPS, 0.0)
    out = out + jnp.where(lane == ROUTE_ID + 1, i2 - N_EXPERT_GROUPS, 0.0)
    return out + jnp.where(lane == ROUTE_W, w1, 0.0) + jnp.where(lane == ROUTE_W + 1, w2, 0.0)


def _out_ln_route_kernel(a_ref, b_ref, wa_ref, wb_ref, x_ref, g1_ref, sc2_ref, sh2_ref,
                         lng_ref, lnb_ref, wrh_ref, wrl_ref, br_ref, x1_ref, h2_ref, route_ref):
    y = jnp.dot(a_ref[0], wa_ref[...], preferred_element_type=F32)
    y = y + jnp.dot(b_ref[0], wb_ref[...], preferred_element_type=F32)
    x1 = _layer_norm(ALPHA * x_ref[0] + g1_ref[0] * y, lng_ref[...], lnb_ref[...])
    x1_ref[0] = x1
    h2 = x1 * (1.0 + sc2_ref[0]) + sh2_ref[0]
    h2_ref[0] = _pack_bf16_pairs(h2)
    lg = _dot_split(h2, wrh_ref[...], wrl_ref[...]) + br_ref[...]
    route_ref[0] = _route(lg)


def _out_ln_route(a, b, wa, wb, x, g1, sc2, sh2, lng, lnb, wrh, wrl, br, tm):
    bsz, n, d = x.shape
    tm = min(tm, n)
    ka, kb = a.shape[-1], b.shape[-1]
    vec = pl.BlockSpec((1, 1, d), lambda bb, i: (bb, 0, 0))
    row = pl.BlockSpec((1, d), lambda bb, i: (0, 0))
    tile = lambda w: pl.BlockSpec((1, tm, w), lambda bb, i: (bb, i, 0))
    full = lambda r, c: pl.BlockSpec((r, c), lambda bb, i: (0, 0))
    return pl.pallas_call(
        _out_ln_route_kernel,
        grid=(bsz, n // tm),
        in_specs=[tile(ka), tile(kb), full(ka, d), full(kb, d), tile(d), vec, vec, vec, row, row,
                  full(d, ROUTER_LANES), full(d, ROUTER_LANES), full(1, ROUTER_LANES)],
        out_specs=[tile(d), tile(d // 2), tile(ROUTER_LANES)],
        out_shape=[jax.ShapeDtypeStruct((bsz, n, d), F32),
                   jax.ShapeDtypeStruct((bsz, n, d // 2), U32),
                   jax.ShapeDtypeStruct((bsz, n, ROUTER_LANES), F32)],
        compiler_params=_params("parallel", "parallel"),
        name="out_ln_route",
    )(a, b, wa, wb, x, g1, sc2, sh2, lng, lnb, wrh, wrl, br)


def _moe_plan(route, n_tiles):
    ids = route[:, ROUTE_ID:ROUTE_ID + TOP_K].astype(jnp.int32)
    flat = ids.T.reshape(-1)
    onehot = (flat[:, None] == jnp.arange(N_EXPERTS, dtype=jnp.int32)[None, :]).astype(jnp.int32)
    csum = jnp.cumsum(onehot, axis=0)
    counts = csum[-1]
    padded = ((counts + EXPERT_TILE - 1) // EXPERT_TILE) * EXPERT_TILE
    ends = jnp.cumsum(padded)
    starts = ends - padded
    pos = jnp.sum((csum - onehot + starts[None, :]) * onehot, axis=1)
    tile_start = jnp.arange(n_tiles, dtype=jnp.int32) * EXPERT_TILE
    tile_expert = jnp.minimum(jnp.sum((tile_start[:, None] >= ends[None, :]).astype(jnp.int32), axis=1),
                              N_EXPERTS - 1)
    used = (ends[-1] // EXPERT_TILE).reshape(1)
    return pos.reshape(TOP_K, -1), ends, tile_expert, used


def _row_copy(src, src_row, dst, dst_row, sem):
    return pltpu.make_async_copy(src.at[pl.ds(src_row, 1), :], dst.at[pl.ds(dst_row, 1), :], sem)


def _dispatch_kernel(ends_ref, pos_ref, h_ref, x_hbm, zero_ref, sem, zsem):
    tm = h_ref.shape[0]

    @pl.when(pl.program_id(0) == 0)
    def _():
        zero_ref[...] = jnp.zeros_like(zero_ref)
        for action in ("start", "wait"):
            for e in range(N_EXPERTS):
                end = ends_ref[e]
                begin = ends_ref[e - 1] if e else 0

                @pl.when(end > begin)
                def _():
                    start = pl.multiple_of(end - EXPERT_TILE, EXPERT_TILE)
                    cp = pltpu.make_async_copy(zero_ref, x_hbm.at[pl.ds(start, EXPERT_TILE), :], zsem)
                    getattr(cp, action)()

    def issue(i, carry):
        for u in range(ISSUE_UNROLL):
            r = i * ISSUE_UNROLL + u
            for k in range(TOP_K):
                _row_copy(h_ref, r, x_hbm, pos_ref[0, 0, k * tm + r], sem).start(priority=k % DMA_THREADS)
        return carry

    def drain(i, carry):
        for _ in range(ISSUE_UNROLL * TOP_K):
            _row_copy(h_ref, 0, x_hbm, 0, sem).wait()
        return carry

    lax.fori_loop(0, tm // ISSUE_UNROLL, issue, 0)
    lax.fori_loop(0, tm // ISSUE_UNROLL, drain, 0)


def _dispatch(h2, pos, ends, n_rows, tm):
    t, d = h2.shape
    pos_blocks = pos.reshape(TOP_K, t // tm, tm).transpose(1, 0, 2).reshape(t // tm, 1, TOP_K * tm)
    return pl.pallas_call(
        _dispatch_kernel,
        grid_spec=pltpu.PrefetchScalarGridSpec(
            num_scalar_prefetch=1,
            grid=(t // tm,),
            in_specs=[pl.BlockSpec((1, 1, TOP_K * tm), lambda i, ends: (i, 0, 0), memory_space=pltpu.SMEM),
                      pl.BlockSpec((tm, d), lambda i, ends: (i, 0))],
            out_specs=pl.BlockSpec(memory_space=pl.ANY),
            scratch_shapes=[pltpu.VMEM((EXPERT_TILE, d), h2.dtype),
                            pltpu.SemaphoreType.DMA, pltpu.SemaphoreType.DMA]),
        out_shape=jax.ShapeDtypeStruct((n_rows, d), h2.dtype),
        compiler_params=_params("arbitrary"),
        name="moe_dispatch",
    )(ends, pos_blocks, h2)


def _expert_kernel(te_ref, used_ref, x_ref, wg_ref, wu_ref, wd_ref, y_ref, wg_bf, wu_bf, wd_bf):
    j = pl.program_id(0)
    live = j < used_ref[0]
    new_expert = (j == 0) | (te_ref[j] != te_ref[jnp.maximum(j - 1, 0)])

    @pl.when(live & new_expert)
    def _():
        wg_bf[...] = wg_ref[0, 0].astype(BF16)
        wu_bf[...] = wu_ref[0, 0].astype(BF16)
        wd_bf[...] = wd_ref[0, 0].astype(BF16)

    @pl.when(live)
    def _():
        x = _unpack_bf16_pairs(x_ref[...]).astype(BF16)
        gate = jnp.dot(x, wg_bf[...], preferred_element_type=F32)
        up = jnp.dot(x, wu_bf[...], preferred_element_type=F32)
        hid = (_silu(gate) * up).astype(BF16)
        y_ref[...] = _pack_bf16_pairs(jnp.dot(hid, wd_bf[...], preferred_element_type=F32))

    @pl.when(jnp.logical_not(live))
    def _():
        y_ref[...] = jnp.zeros_like(y_ref)


def _experts(xs, tile_expert, used, wg, wu, wd, layer):
    n_rows, half = xs.shape
    d = 2 * half
    n_tiles = n_rows // EXPERT_TILE
    by_expert = lambda j, te, used: (layer, te[j], 0, 0)
    return pl.pallas_call(
        _expert_kernel,
        grid_spec=pltpu.PrefetchScalarGridSpec(
            num_scalar_prefetch=2,
            grid=(n_tiles,),
            in_specs=[pl.BlockSpec((EXPERT_TILE, half), lambda j, te, used: (jnp.minimum(j, used[0] - 1), 0)),
                      pl.BlockSpec((1, 1, d, D_EXPERT), by_expert),
                      pl.BlockSpec((1, 1, d, D_EXPERT), by_expert),
                      pl.BlockSpec((1, 1, D_EXPERT, d), by_expert)],
            out_specs=pl.BlockSpec((EXPERT_TILE, half), lambda j, te, used: (j, 0)),
            scratch_shapes=[pltpu.VMEM((d, D_EXPERT), BF16), pltpu.VMEM((d, D_EXPERT), BF16),
                            pltpu.VMEM((D_EXPERT, d), BF16)]),
        out_shape=jax.ShapeDtypeStruct((n_rows, half), U32),
        compiler_params=_params("arbitrary"),
        name="moe_experts",
    )(tile_expert, used, xs, wg, wu, wd)


def _combine_kernel(pos_ref, next_pos_ref, route_ref, x1_ref, g2_ref, lng_ref, lnb_ref, y_hbm, o_ref, ybuf, sems):
    tm = x1_ref.shape[0]
    i = pl.program_id(0)
    slot = i % 2

    def gather(idx_ref, s):
        def issue(it, carry):
            for u in range(ISSUE_UNROLL):
                r = it * ISSUE_UNROLL + u
                for k in range(TOP_K):
                    _row_copy(y_hbm, idx_ref[0, 0, k * tm + r], ybuf.at[s, k], r,
                              sems.at[s]).start(priority=k % DMA_THREADS)
            return carry

        lax.fori_loop(0, tm // ISSUE_UNROLL, issue, 0)

    @pl.when(i == 0)
    def _():
        gather(pos_ref, slot)

    @pl.when(i + 1 < pl.num_programs(0))
    def _():
        gather(next_pos_ref, 1 - slot)

    def drain(it, carry):
        for _ in range(ISSUE_UNROLL * TOP_K):
            _row_copy(y_hbm, 0, ybuf.at[slot, 0], 0, sems.at[slot]).wait()
        return carry

    lax.fori_loop(0, tm // ISSUE_UNROLL, drain, 0)

    for c in range(tm // COMBINE_ROWS):
        rows = pl.ds(c * COMBINE_ROWS, COMBINE_ROWS)
        route = route_ref[rows, :]
        lane = lax.broadcasted_iota(jnp.int32, route.shape, 1)
        y = jnp.zeros((COMBINE_ROWS, x1_ref.shape[1]), F32)
        for k in range(TOP_K):
            w = jnp.sum(jnp.where(lane == ROUTE_W + k, route, 0.0), axis=-1, keepdims=True)
            y = y + w * _unpack_bf16_pairs(ybuf[slot, k, rows, :])
        z = ALPHA * x1_ref[rows, :] + g2_ref[0] * y
        o_ref[rows, :] = _layer_norm(z, lng_ref[...], lnb_ref[...])


def _combine(y, pos, route, x1, g2, lng, lnb, tiles_per_batch, tm):
    t, d = x1.shape
    steps = t // tm
    pos_blocks = pos.reshape(TOP_K, steps, tm).transpose(1, 0, 2).reshape(steps, 1, TOP_K * tm)
    return pl.pallas_call(
        _combine_kernel,
        grid=(steps,),
        in_specs=[pl.BlockSpec((1, 1, TOP_K * tm), lambda i: (i, 0, 0), memory_space=pltpu.SMEM),
                  pl.BlockSpec((1, 1, TOP_K * tm), lambda i: (jnp.minimum(i + 1, steps - 1), 0, 0),
                               memory_space=pltpu.SMEM),
                  pl.BlockSpec((tm, ROUTER_LANES), lambda i: (i, 0)),
                  pl.BlockSpec((tm, d), lambda i: (i, 0)),
                  pl.BlockSpec((1, 1, d), lambda i: (i // tiles_per_batch, 0, 0)),
                  pl.BlockSpec((1, d), lambda i: (0, 0)),
                  pl.BlockSpec((1, d), lambda i: (0, 0)),
                  pl.BlockSpec(memory_space=pl.ANY)],
        out_specs=pl.BlockSpec((tm, d), lambda i: (i, 0)),
        out_shape=jax.ShapeDtypeStruct((t, d), F32),
        scratch_shapes=[pltpu.VMEM((2, TOP_K, tm, d // 2), U32), pltpu.SemaphoreType.DMA((2,))],
        compiler_params=_params("arbitrary"),
        name="moe_combine",
    )(pos_blocks, pos_blocks, route, x1, g2, lng, lnb, y)


def _moe(h2, route, experts, layer, x1, g2, lng, lnb):
    bsz, n, d = x1.shape
    t = bsz * n
    tm = min(MOE_TOKEN_TILE, n)
    n_tiles = TOP_K * t // EXPERT_TILE + N_EXPERTS
    route = route.reshape(t, ROUTER_LANES)
    pos, ends, tile_expert, used = _moe_plan(route, n_tiles)
    xs = _dispatch(h2.reshape(t, d // 2), pos, ends, n_tiles * EXPERT_TILE, tm)
    y = _experts(xs, tile_expert, used, *experts, layer)
    out = _combine(y, pos, route, x1.reshape(t, d), g2, lng, lnb, n // tm, tm)
    return out.reshape(bsz, n, d)


def _gelu_tanh(x):
    return 0.5 * x * (1.0 + jnp.tanh(np.sqrt(2.0 / np.pi).astype(np.float32) * (x + 0.044715 * (x * x * x))))


def _cd_in_kernel(x_ref, sc_ref, sh_ref, w_ref, sg_ref, sb_ref, wsp_ref, bsp_ref, yc_ref, glu_ref):
    tm = x_ref.shape[1]
    h = (x_ref[0] * (1.0 + sc_ref[0]) + sh_ref[0]).astype(BF16)
    p = jnp.dot(h, w_ref[...], preferred_element_type=F32)
    for g in range(N_SGU_GROUPS):
        u = _gelu_tanh(p[:, g * SGU_GROUP_DIM:(g + 1) * SGU_GROUP_DIM])
        v = _gelu_tanh(p[:, C_W + g * SGU_GROUP_DIM:C_W + (g + 1) * SGU_GROUP_DIM])
        vg = _layer_norm(v, sg_ref[g:g + 1, :], sb_ref[g:g + 1, :]).astype(BF16)
        for c in range(tm // CHUNK):
            rows = slice(c * CHUNK, (c + 1) * CHUNK)
            sv = jnp.dot(wsp_ref[g], vg[rows], preferred_element_type=F32) + bsp_ref[g]
            yc_ref[0, rows, g * SGU_GROUP_DIM:(g + 1) * SGU_GROUP_DIM] = (u[rows] * sv).astype(BF16)
    a = p[:, 2 * C_W:2 * C_W + D_W]
    gate = p[:, 2 * C_W + D_W:]
    glu_ref[0] = a * jax.nn.sigmoid(gate)


def _cd_in_proj(x, sc, sh, w_bf, sg, sb, wsp_bf, bsp, tm):
    bsz, n, d = x.shape
    tm = min(tm, n)
    vec = pl.BlockSpec((1, 1, d), lambda b, i: (b, 0, 0))
    return pl.pallas_call(
        _cd_in_kernel,
        grid=(bsz, n // tm),
        in_specs=[pl.BlockSpec((1, tm, d), lambda b, i: (b, i, 0)), vec, vec,
                  pl.BlockSpec((d, CD_IN), lambda b, i: (0, 0)),
                  pl.BlockSpec((N_SGU_GROUPS, SGU_GROUP_DIM), lambda b, i: (0, 0)),
                  pl.BlockSpec((N_SGU_GROUPS, SGU_GROUP_DIM), lambda b, i: (0, 0)),
                  pl.BlockSpec((N_SGU_GROUPS, CHUNK, CHUNK), lambda b, i: (0, 0, 0)),
                  pl.BlockSpec((N_SGU_GROUPS, CHUNK, SGU_GROUP_DIM), lambda b, i: (0, 0, 0))],
        out_specs=[pl.BlockSpec((1, tm, C_W), lambda b, i: (b, i, 0)),
                   pl.BlockSpec((1, tm, D_W), lambda b, i: (b, i, 0))],
        out_shape=[jax.ShapeDtypeStruct((bsz, n, C_W), BF16),
                   jax.ShapeDtypeStruct((bsz, n, D_W), F32)],
        compiler_params=_params("parallel", "parallel"),
        name="cd_in_proj",
    )(x, sc, sh, w_bf, sg, sb, wsp_bf, bsp)


def _conv_kernel(glu_ref, w_ref, cb_ref, g_ref, b_ref, o_ref, pad_ref, *, rows):
    n = glu_ref.shape[1]
    halo = jnp.zeros((CONV_HALO, CONV_GROUP_DIM), F32)
    pad_ref[0:CONV_HALO, :] = halo
    pad_ref[CONV_HALO + n:CONV_HALO + n + CONV_HALO, :] = halo
    pad_ref[CONV_HALO:CONV_HALO + n, :] = glu_ref[0]
    first = CONV_HALO - CONV_WIDTH // 2
    for c in range(n // rows):
        acc = jnp.zeros((rows, CONV_GROUP_DIM), F32) + cb_ref[...]
        for k in range(CONV_WIDTH):
            start = c * rows + first + k
            acc = acc + w_ref[k:k + 1, :] * pad_ref[start:start + rows, :]
        dn = _layer_norm(acc, g_ref[...], b_ref[...])
        o_ref[0, c * rows:(c + 1) * rows, :] = _silu(dn).astype(BF16)


def _conv_module(glu, conv_w, conv_b, cn_g, cn_b, rows):
    bsz, n, _ = glu.shape
    rows = min(rows, n)
    vec = pl.BlockSpec((1, CONV_GROUP_DIM), lambda b, g: (0, g))
    return pl.pallas_call(
        functools.partial(_conv_kernel, rows=rows),
        grid=(bsz, N_CONV_GROUPS),
        in_specs=[pl.BlockSpec((1, n, CONV_GROUP_DIM), lambda b, g: (b, 0, g)),
                  pl.BlockSpec((CONV_WIDTH, CONV_GROUP_DIM), lambda b, g: (0, g)),
                  vec, vec, vec],
        out_specs=pl.BlockSpec((1, n, CONV_GROUP_DIM), lambda b, g: (b, 0, g)),
        out_shape=jax.ShapeDtypeStruct((bsz, n, D_W), BF16),
        scratch_shapes=[pltpu.VMEM((n + 2 * CONV_HALO, CONV_GROUP_DIM), F32)],
        compiler_params=_params("parallel", "parallel"),
        name="conv_module",
    )(glu, conv_w, conv_b, cn_g, cn_b)


def _head_pair_columns():
    inner = np.concatenate([np.arange(0, HEAD_DIM, 2), np.arange(1, HEAD_DIM, 2)])
    cols = []
    for t in range(N_KV_TILES):
        for r in range(REP):
            for g in (2 * t, 2 * t + 1):
                cols.append((g * REP + r) * HEAD_DIM + inner)
    for g in range(N_KV_HEADS):
        cols.append(Q_W + g * HEAD_DIM + inner)
    cols.append(np.arange(Q_W + KV_W, AB_IN))
    return np.concatenate(cols)


def _attn_out_rows():
    rows = []
    for t in range(N_KV_TILES):
        for r in range(REP):
            for g in (2 * t, 2 * t + 1):
                rows.append((g * REP + r) * HEAD_DIM + np.arange(HEAD_DIM))
    return np.concatenate(rows)


def _rope_tables(n, rotate):
    if not rotate:
        return jnp.ones((n, LANES), F32), jnp.zeros((n, LANES), F32)
    rows = n // GRID_W
    row = jnp.repeat(jnp.arange(rows, dtype=F32), GRID_W)
    col = jnp.tile(jnp.arange(GRID_W, dtype=F32), rows)
    inv_freq = 1.0 / (ROPE_THETA ** (jnp.arange(0, AXIS_DIM, 2, dtype=F32) / AXIS_DIM))
    ang = jnp.concatenate([row[:, None] * inv_freq, col[:, None] * inv_freq], axis=-1)
    cos, sin = jnp.cos(ang), jnp.sin(ang)
    return jnp.tile(cos, (1, 4)), jnp.tile(jnp.concatenate([-sin, sin], axis=-1), (1, 2))


def _pair_gain(g):
    perm = np.concatenate([np.arange(0, HEAD_DIM, 2), np.arange(1, HEAD_DIM, 2)])
    return jnp.tile(g[perm], 2).reshape(1, LANES)


def _score_bound(qg, kg):
    bound = 1.02 * HEAD_DIM * Q_SCALE * jnp.max(jnp.abs(qg)) * jnp.max(jnp.abs(kg))
    return bound.reshape(1).astype(F32)


def _block_diag(w):
    g, a, b = w.shape
    out = jnp.zeros((g * a, g * b), w.dtype)
    for i in range(g):
        out = out.at[i * a:(i + 1) * a, i * b:(i + 1) * b].set(w[i])
    return out


def _router_weights(w_group, b_group, w_router, b_router):
    d = w_group.shape[0]
    wr = jnp.concatenate([w_group, jnp.transpose(w_router, (1, 0, 2)).reshape(d, N_EXPERTS)], axis=1)
    br = jnp.concatenate([b_group, b_router.reshape(N_EXPERTS)])
    pad = ROUTER_LANES - wr.shape[1]
    wr = jnp.pad(wr, ((0, 0), (0, pad)))
    wr_hi = wr.astype(BF16)
    wr_lo = (wr - wr_hi.astype(F32)).astype(BF16)
    return wr_hi, wr_lo, jnp.pad(br, (0, pad)).reshape(1, ROUTER_LANES)


def _vec(mod_l, k, rows):
    first, count, bsz = rows
    v = mod_l[first:first + count, k * D_MODEL:(k + 1) * D_MODEL][:, None, :]
    return jnp.broadcast_to(v, (bsz, 1, D_MODEL))


def _channel_mix(y_a, y_b, wa, wb, x, mod_l, rows, ln_g, ln_b, router, experts, layer, tm):
    lng = lambda i: ln_g[i].reshape(1, D_MODEL)
    lnb = lambda i: ln_b[i].reshape(1, D_MODEL)
    x1, h2, route = _out_ln_route(y_a, y_b, wa, wb, x, _vec(mod_l, 2, rows), _vec(mod_l, 4, rows),
                                  _vec(mod_l, 3, rows), lng(0), lnb(0), *router, tm=tm)
    return _moe(h2, route, experts, layer, x1, _vec(mod_l, 5, rows), lng(1), lnb(1))


def kernel(x, c, ctx, c_ctx, w_mod, b_mod, ln_g, ln_b, w_in_ab, w_out_ab, q_gain, k_gain, w_fourier, b_fourier, w_in_cd, w_out_cd, sgu_g, sgu_b, w_spatial, b_spatial, conv_w, conv_b, conv_norm_g, conv_norm_b, w_group, b_group, w_router, b_router, w_exp_gate, w_exp_up, w_exp_down):
    bsz, n, _ = x.shape
    lc = ctx.shape[1]
    depth = w_mod.shape[0]
    pad_rows = (-(bsz + 1)) % 8
    cond = jnp.concatenate([c, c_ctx[None, :], jnp.zeros((pad_rows, D_MODEL), F32)], axis=0)
    mod = _modulation(cond, w_mod, b_mod)
    lat_rows = (0, bsz, bsz)
    ctx_rows = (bsz, 1, bsz)

    in_cols = _head_pair_columns()
    out_rows = _attn_out_rows()
    rope_lat = _rope_tables(n, True)
    rope_ctx = _rope_tables(lc, False)
    dft = {m: _dft_tables(m) for m in {n, lc}}

    experts = (w_exp_gate, w_exp_up, w_exp_down)
    x_lat, x_ctx = x, ctx
    for l in range(depth):
        ctx_out = any(j % 2 == 0 for j in range(l + 1, depth))
        i = l // 2
        mod_l = mod[l]
        router = _router_weights(w_group[l], b_group[l], w_router[l], b_router[l])
        if l % 2 == 0:
            w_in = w_in_ab[i][:, in_cols].astype(BF16)
            w_out = w_out_ab[i].astype(BF16)
            wa, wb = w_out[out_rows], w_out[Q_W:]
            qg, kg = _pair_gain(q_gain[i]), _pair_gain(k_gain[i])
            bound = _score_bound(q_gain[i], k_gain[i])
            wf = _block_diag(w_fourier[i]).astype(BF16)
            bf = b_fourier[i].reshape(1, F_W)
            q, k, v, f = _ab_in_proj(x_lat, _vec(mod_l, 1, lat_rows), _vec(mod_l, 0, lat_rows),
                                     w_in, *rope_lat, qg, kg, tm=512)
            qc, kc, vc, fc = _ab_in_proj(x_ctx, _vec(mod_l, 1, ctx_rows), _vec(mod_l, 0, ctx_rows),
                                         w_in, *rope_ctx, qg, kg, tm=256)
            a_lat = _attention(q, [kc, k], [vc, v], bound, tq=512)
            cs, wc = dft[n]
            fm_lat = _fourier_mix(f, jnp.asarray(wc).astype(BF16), jnp.asarray(cs).astype(BF16), wf, bf, tr=1024)
            ya_lat, yb_lat = a_lat, fm_lat
            if ctx_out:
                ya_ctx = _attention(qc, [kc], [vc], bound, tq=256)
                cs, wc = dft[lc]
                yb_ctx = _fourier_mix(fc, jnp.asarray(wc).astype(BF16), jnp.asarray(cs).astype(BF16), wf, bf, tr=256)
        else:
            w_in = w_in_cd[i].astype(BF16)
            w_out = w_out_cd[i].astype(BF16)
            wa, wb = w_out[:C_W], w_out[C_W:]
            wsp = w_spatial[i].astype(BF16)
            bsp = jnp.broadcast_to(b_spatial[i][:, :, None], (N_SGU_GROUPS, CHUNK, SGU_GROUP_DIM))
            cd = (conv_w[i], conv_b[i].reshape(1, D_W), conv_norm_g[i].reshape(1, D_W), conv_norm_b[i].reshape(1, D_W))
            ya_lat, glu = _cd_in_proj(x_lat, _vec(mod_l, 1, lat_rows), _vec(mod_l, 0, lat_rows),
                                      w_in, sgu_g[i], sgu_b[i], wsp, bsp, tm=256)
            yb_lat = _conv_module(glu, *cd, rows=256)
            if ctx_out:
                ya_ctx, glu_c = _cd_in_proj(x_ctx, _vec(mod_l, 1, ctx_rows), _vec(mod_l, 0, ctx_rows),
                                            w_in, sgu_g[i], sgu_b[i], wsp, bsp, tm=256)
                yb_ctx = _conv_module(glu_c, *cd, rows=256)
        x_lat = _channel_mix(ya_lat, yb_lat, wa, wb, x_lat, mod_l, lat_rows, ln_g[l], ln_b[l], router, experts, l, tm=512)
        if ctx_out:
            x_ctx = _channel_mix(ya_ctx, yb_ctx, wa, wb, x_ctx, mod_l, ctx_rows, ln_g[l], ln_b[l], router, experts, l, tm=256)
        else:
            x_ctx = None
    return x_lat
```
